```python
import jax, jax.numpy as jnp
from jax import lax
import numpy as np

D_MODEL = 4096
BATCH = 1
SEQ = 16384
DEPTH = 1
DEC_BATCH = 32
DEC_SEQ = 64
PAST_LEN = 4096

CHUNK = 64
SGU_CHUNK = 128
A_WIDTH = D_MODEL // 2
A_HEADS = 16
A_HEAD_DIM = A_WIDTH // A_HEADS
B_WIDTH = D_MODEL - A_WIDTH
B_HEADS = 4
B_KEY_DIM = B_WIDTH // 2
B_HEAD_K = B_KEY_DIM // B_HEADS
B_HEAD_V = B_WIDTH // B_HEADS
GATE_RANK = 16
GATE_TAU = 16.0
D_FF = -(-8 * D_MODEL // (3 * 256)) * 256
EPS = 1e-6
IN_COLS = 2 * A_WIDTH + 2 * B_KEY_DIM + 2 * B_WIDTH + GATE_RANK

kernel_name = "hybrid_sgu_gla_stream_step"


def rmsnorm(x, g):
    xf = x.astype(jnp.float32)
    y = xf * lax.rsqrt(jnp.mean(xf * xf, axis=-1, keepdims=True) + EPS)
    return (y * g.astype(jnp.float32)).astype(x.dtype)


def layernorm(x, g, b):
    xf = x.astype(jnp.float32)
    mu = jnp.mean(xf, axis=-1, keepdims=True)
    var = jnp.mean(jnp.square(xf - mu), axis=-1, keepdims=True)
    y = (xf - mu) * lax.rsqrt(var + EPS) * g.astype(jnp.float32) + b.astype(jnp.float32)
    return y.astype(x.dtype)


def sgu(z_u, z_v, w_s, b_s, ln_g, ln_b):
    bsz, L, _ = z_u.shape
    c = min(SGU_CHUNK, L)
    n = L // c
    v_n = layernorm(z_v, ln_g, ln_b)
    vv = v_n.reshape(bsz, n, c, A_HEADS, A_HEAD_DIM)
    w = w_s[:, :c, :c] * jnp.tril(jnp.ones((c, c), w_s.dtype))
    mixed = jnp.einsum('hij,bnjhd->bnihd', w, vv) + b_s[:, :c].T[None, None, :, :, None]
    return z_u * mixed.reshape(bsz, L, A_WIDTH), v_n


def gla(q, k, v, log_a, S0):
    bsz, L, H, dk = q.shape
    dv = v.shape[-1]
    c = min(CHUNK, L)
    n = L // c

    def to_blocks(t):
        return t.reshape(bsz, n, c, H, t.shape[-1]).transpose(1, 0, 3, 2, 4)

    mask = jnp.tril(jnp.ones((c, c), dtype=bool))
    ref = c // 2

    def step(S, inp):
        qc, kc, vc, gc = inp
        b = jnp.cumsum(gc, axis=2)
        b_ref = b[:, :, ref:ref + 1]
        b_last = b[:, :, -1:]
        att = jnp.einsum('bhid,bhjd->bhij', qc * jnp.exp(b - b_ref), kc * jnp.exp(b_ref - b))
        att = jnp.where(mask, att, 0.0)
        o = jnp.einsum('bhij,bhje->bhie', att, vc) + jnp.einsum('bhid,bhde->bhie', qc * jnp.exp(b), S)
        S_new = jnp.exp(b_last)[:, :, 0, :, None] * S + jnp.einsum('bhjd,bhje->bhde', kc * jnp.exp(b_last - b), vc)
        return S_new, o

    xs = (to_blocks(q), to_blocks(k), to_blocks(v), to_blocks(log_a))
    S_fin, o = lax.scan(step, S0.astype(jnp.float32), xs)
    o = o.transpose(1, 0, 3, 2, 4).reshape(bsz, L, H, dv)
    return o.astype(q.dtype), S_fin.astype(S0.dtype)


def mixer(h, S0, w_in, w_s, b_s, ln_g, ln_b, w_gate_up, b_gate, gla_norm_g, w_out):
    bsz, L, _ = h.shape
    proj = jnp.einsum('bld,de->ble', h, w_in)
    offs = np.cumsum([0, A_WIDTH, A_WIDTH, B_KEY_DIM, B_KEY_DIM, B_WIDTH, B_WIDTH, GATE_RANK])
    a_u, a_v, q, k, v, r, g_lr = [proj[..., int(offs[i]):int(offs[i + 1])] for i in range(7)]
    a_out, v_rows = sgu(jax.nn.gelu(a_u), jax.nn.gelu(a_v), w_s, b_s, ln_g, ln_b)
    q = q.reshape(bsz, L, B_HEADS, B_HEAD_K) * (B_HEAD_K ** -0.5)
    k = k.reshape(bsz, L, B_HEADS, B_HEAD_K)
    v = v.reshape(bsz, L, B_HEADS, B_HEAD_V)
    z = (jnp.einsum('blr,rk->blk', g_lr, w_gate_up) + b_gate).astype(jnp.float32)
    log_a = (jax.nn.log_sigmoid(z) / GATE_TAU).reshape(bsz, L, B_HEADS, B_HEAD_K)
    o, S_new = gla(q, k, v, log_a, S0)
    o = rmsnorm(o, gla_norm_g).reshape(bsz, L, B_WIDTH)
    b_out = o * jax.nn.silu(r)
    mix = jnp.einsum('ble,ed->bld', jnp.concatenate([a_out, b_out], axis=-1), w_out)
    return mix, v_rows, S_new


def swiglu(h, w_gate, w_up, w_down):
    return jnp.einsum('blf,fd->bld', jax.nn.silu(h @ w_gate) * (h @ w_up), w_down)


def trunk(x, states, g_mix, w_in, w_s, b_s, ln_g, ln_b, w_gate_up, b_gate, gla_norm_g, w_out,
          g_ffn, w_ffn_gate, w_ffn_up, w_ffn_down, g_final):
    new_states, v_rows_all = [], []
    for d in range(DEPTH):
        mix, v_rows, S_new = mixer(rmsnorm(x, g_mix[d]), states[d], w_in[d], w_s[d], b_s[d], ln_g[d],
                                   ln_b[d], w_gate_up[d], b_gate[d], gla_norm_g[d], w_out[d])
        x = x + mix
        x = x + swiglu(rmsnorm(x, g_ffn[d]), w_ffn_gate[d], w_ffn_up[d], w_ffn_down[d])
        new_states.append(S_new)
        v_rows_all.append(v_rows)
    return rmsnorm(x, g_final), jnp.stack(new_states), jnp.stack(v_rows_all)


def setup_inputs(seed: int = 0) -> dict:
    key = jax.random.key(seed)
    ks = jax.random.split(key, 24)
    nrm = lambda k, shape, s: jax.random.normal(k, shape, jnp.float32) * s
    return {
        "x_prompt": nrm(ks[0], (BATCH, SEQ, D_MODEL), 1.0),
        "x_sample": nrm(ks[1], (DEC_BATCH, DEC_SEQ, D_MODEL), 1.0),
        "state_gla": nrm(ks[2], (DEPTH, DEC_BATCH, B_HEADS, B_HEAD_K, B_HEAD_V), 0.5),
        "g_mix": 1.0 + nrm(ks[3], (DEPTH, D_MODEL), 0.02),
        "w_in": nrm(ks[4], (DEPTH, D_MODEL, IN_COLS), D_MODEL ** -0.5),
        "w_s": nrm(ks[5], (DEPTH, A_HEADS, SGU_CHUNK, SGU_CHUNK), SGU_CHUNK ** -0.5),
        "b_s": nrm(ks[6], (DEPTH, A_HEADS, SGU_CHUNK), 0.1),
        "ln_g": 1.0 + nrm(ks[7], (DEPTH, A_WIDTH), 0.02),
        "ln_b": nrm(ks[8], (DEPTH, A_WIDTH), 0.02),
        "w_gate_up": nrm(ks[9], (DEPTH, GATE_RANK, B_KEY_DIM), GATE_RANK ** -0.5),
        "b_gate": nrm(ks[10], (DEPTH, B_KEY_DIM), 0.1),
        "gla_norm_g": 1.0 + nrm(ks[11], (DEPTH, B_HEAD_V), 0.02),
        "w_out": nrm(ks[12], (DEPTH, D_MODEL, D_MODEL), D_MODEL ** -0.5),
        "g_ffn": 1.0 + nrm(ks[13], (DEPTH, D_MODEL), 0.02),
        "w_ffn_gate": nrm(ks[14], (DEPTH, D_MODEL, D_FF), D_MODEL ** -0.5),
        "w_ffn_up": nrm(ks[15], (DEPTH, D_MODEL, D_FF), D_MODEL ** -0.5),
        "w_ffn_down": nrm(ks[16], (DEPTH, D_FF, D_MODEL), D_FF ** -0.5),
        "g_final": 1.0 + nrm(ks[17], (D_MODEL,), 0.02),
    }


def reference(x_prompt, x_sample, state_gla, g_mix, w_in, w_s, b_s, ln_g, ln_b, w_gate_up, b_gate,
              gla_norm_g, w_out, g_ffn, w_ffn_gate, w_ffn_up, w_ffn_down, g_final):
    params = (g_mix, w_in, w_s, b_s, ln_g, ln_b, w_gate_up, b_gate, gla_norm_g, w_out,
              g_ffn, w_ffn_gate, w_ffn_up, w_ffn_down, g_final)
    zero_state = jnp.zeros((DEPTH, x_prompt.shape[0], B_HEADS, B_HEAD_K, B_HEAD_V), state_gla.dtype)
    y_prompt, new_gla_prompt, _ = trunk(x_prompt, zero_state, *params)
    y_sample, new_gla_sample, new_sgu_v_sample = trunk(x_sample, state_gla, *params)
    return (y_prompt, y_sample, new_gla_prompt, new_gla_sample, new_sgu_v_sample)
```

```python
import functools
import math

import jax
import jax.numpy as jnp
from jax import lax
from jax.experimental import pallas as pl
from jax.experimental.pallas import tpu as pltpu

F32 = jnp.float32
BF16 = jnp.bfloat16

EPS = 1e-6
GLA_BLOCK = 64
GATE_TAU = 16.0
LANES = 128
V7X_VMEM_LIMIT_BYTES = 56 * 1024 * 1024


def _pick(n, candidates):
    for c in candidates:
        if n % c == 0:
            return c
    raise ValueError(f"no tile in {candidates} divides {n}")


def _round_up(n, m):
    return -(-n // m) * m


def _params(*semantics):
    return pltpu.CompilerParams(dimension_semantics=semantics,
                                vmem_limit_bytes=V7X_VMEM_LIMIT_BYTES)


def _dot(a, b):
    return jnp.dot(a, b, preferred_element_type=F32)


def _gelu_tanh(x):
    c = math.sqrt(2.0 / math.pi)
    return 0.5 * x * (1.0 + jnp.tanh(c * (x + 0.044715 * (x * x * x))))


def _silu(x):
    return x * (1.0 / (1.0 + jnp.exp(-x)))


def _rms(x, g):
    return x * lax.rsqrt(jnp.mean(x * x, axis=-1, keepdims=True) + EPS) * g


def _prep_kernel(x_ref, g_ref, wlr_ref, wup_ref, bg_ref, h_ref, b_ref):
    rows = x_ref.shape[0]
    hb = _rms(x_ref[...], g_ref[...]).astype(BF16)
    h_ref[...] = hb
    g_lr = _dot(hb, wlr_ref[...])
    z = _dot(g_lr.astype(BF16), wup_ref[...]) + bg_ref[...]
    log_a = (jnp.minimum(z, 0.0) - jnp.log1p(jnp.exp(-jnp.abs(z)))) * (1.0 / GATE_TAU)
    r = lax.broadcasted_iota(jnp.int32, (rows, rows), 0)
    c = lax.broadcasted_iota(jnp.int32, (rows, rows), 1)
    tri = jnp.where((c <= r) & (r // GLA_BLOCK == c // GLA_BLOCK), 1.0, 0.0).astype(BF16)
    hi = log_a.astype(BF16)
    r1 = log_a - hi.astype(F32)
    mid = r1.astype(BF16)
    lo = (r1 - mid.astype(F32)).astype(BF16)
    b_ref[...] = _dot(tri, hi) + _dot(tri, mid) + _dot(tri, lo)


def _prep(x, g, w_lr, w_up, b_gate):
    n, d = x.shape
    key_dim = w_up.shape[1]
    bm = _pick(n, (512, 256, 128, 64))
    return pl.pallas_call(
        _prep_kernel,
        grid=(n // bm,),
        in_specs=[
            pl.BlockSpec((bm, d), lambda i: (i, 0)),
            pl.BlockSpec((1, d), lambda i: (0, 0)),
            pl.BlockSpec(w_lr.shape, lambda i: (0, 0)),
            pl.BlockSpec(w_up.shape, lambda i: (0, 0)),
            pl.BlockSpec((1, key_dim), lambda i: (0, 0)),
        ],
        out_specs=[
            pl.BlockSpec((bm, d), lambda i: (i, 0)),
            pl.BlockSpec((bm, key_dim), lambda i: (i, 0)),
        ],
        out_shape=[
            jax.ShapeDtypeStruct((n, d), BF16),
            jax.ShapeDtypeStruct((n, key_dim), F32),
        ],
        compiler_params=_params("parallel"),
        name="prep_rmsnorm_gate",
    )(x, g, w_lr, w_up, b_gate)


def _gemm_kernel(a_ref, w_ref, o_ref):
    o_ref[...] = _dot(a_ref[...], w_ref[...]).astype(o_ref.dtype)


def _gemm(a, w, out_dtype):
    m, k = a.shape
    n = w.shape[1]
    bm = _pick(m, (1024, 512, 256, 128, 64))
    bn = _pick(n, (1024, 512, 256, 128))
    return pl.pallas_call(
        _gemm_kernel,
        grid=(m // bm, n // bn),
        in_specs=[
            pl.BlockSpec((bm, k), lambda i, j: (i, 0)),
            pl.BlockSpec((k, bn), lambda i, j: (0, j)),
        ],
        out_specs=pl.BlockSpec((bm, bn), lambda i, j: (i, j)),
        out_shape=jax.ShapeDtypeStruct((m, n), out_dtype),
        compiler_params=_params("parallel", "parallel"),
        name="gemm_in_proj",
    )(a, w)


def _sgu_kernel(u_ref, v_ref, w_ref, bias_ref, lng_ref, lnb_ref, o_ref, *vn_refs,
                chunk, heads, head_dim):
    rows = u_ref.shape[0]
    n_chunks = rows // chunk
    v = _gelu_tanh(v_ref[...].astype(F32))
    mu = jnp.mean(v, axis=-1, keepdims=True)
    vc = v - mu
    var = jnp.mean(vc * vc, axis=-1, keepdims=True)
    vn = vc * lax.rsqrt(var + EPS) * lng_ref[...] + lnb_ref[...]
    if vn_refs:
        vn_refs[0][...] = vn
    vnb = vn.astype(BF16)
    ri = lax.broadcasted_iota(jnp.int32, (chunk, chunk), 0)
    ci = lax.broadcasted_iota(jnp.int32, (chunk, chunk), 1)
    causal = ci <= ri
    for h in range(heads):
        lo, hi = h * head_dim, (h + 1) * head_dim
        w = jnp.where(causal, w_ref[h], 0.0).astype(BF16)
        rhs = jnp.concatenate(
            [vnb[c * chunk:(c + 1) * chunk, lo:hi] for c in range(n_chunks)], axis=1)
        mixed = _dot(w, rhs)
        bias = bias_ref[:, lo:hi]
        for c in range(n_chunks):
            rs = slice(c * chunk, (c + 1) * chunk)
            u = _gelu_tanh(u_ref[rs, lo:hi].astype(F32))
            m = mixed[:, c * head_dim:(c + 1) * head_dim] + bias
            o_ref[rs, lo:hi] = (u * m).astype(o_ref.dtype)


def _sgu(proj, row0, n_rows, w_s, bias_full, ln_g, ln_b, *, chunk, emit_vn):
    heads = w_s.shape[0]
    width = ln_g.shape[1]
    head_dim = width // heads
    rows = _pick(n_rows, (2 * chunk, chunk))
    r0 = row0 // rows
    kern = functools.partial(_sgu_kernel, chunk=chunk, heads=heads, head_dim=head_dim)
    out_specs = [pl.BlockSpec((rows, width), lambda i: (i, 0))]
    out_shape = [jax.ShapeDtypeStruct((n_rows, width), BF16)]
    if emit_vn:
        out_specs.append(pl.BlockSpec((rows, width), lambda i: (i, 0)))
        out_shape.append(jax.ShapeDtypeStruct((n_rows, width), F32))
    return pl.pallas_call(
        kern,
        grid=(n_rows // rows,),
        in_specs=[
            pl.BlockSpec((rows, width), lambda i: (r0 + i, 0)),
            pl.BlockSpec((rows, width), lambda i: (r0 + i, 1)),
            pl.BlockSpec((heads, chunk, chunk), lambda i: (0, 0, 0)),
            pl.BlockSpec((chunk, width), lambda i: (0, 0)),
            pl.BlockSpec((1, width), lambda i: (0, 0)),
            pl.BlockSpec((1, width), lambda i: (0, 0)),
        ],
        out_specs=out_specs,
        out_shape=out_shape,
        compiler_params=_params("parallel"),
        name="sgu_mixer",
    )(proj, proj, w_s, bias_full, ln_g, ln_b)


def _gla_kernel(q_ref, k_ref, v_ref, r_ref, b_ref, s0_ref, gn_ref, o_ref, s_out_ref, s_ref,
                *, heads, dk, dv, blocks):
    t = pl.program_id(1)

    @pl.when(t == 0)
    def _():
        s_ref[...] = s0_ref[0]

    scale = dk ** -0.5
    ri = lax.broadcasted_iota(jnp.int32, (GLA_BLOCK, GLA_BLOCK), 0)
    ci = lax.broadcasted_iota(jnp.int32, (GLA_BLOCK, GLA_BLOCK), 1)
    causal = ci <= ri
    half = GLA_BLOCK // 2
    for blk in range(blocks):
        rs = slice(blk * GLA_BLOCK, (blk + 1) * GLA_BLOCK)
        for h in range(heads):
            ks = slice(h * dk, (h + 1) * dk)
            vs = slice(h * dv, (h + 1) * dv)
            q = q_ref[rs, ks].astype(F32) * scale
            k = k_ref[rs, ks].astype(F32)
            v = v_ref[rs, vs]
            b = b_ref[rs, ks]
            b_mid = b[half:half + 1, :]
            b_last = b[GLA_BLOCK - 1:GLA_BLOCK, :]
            q_in = (q * jnp.exp(b - b_mid)).astype(BF16)
            k_in = (k * jnp.exp(b_mid - b)).astype(BF16)
            att = lax.dot_general(q_in, k_in, (((1,), (1,)), ((), ())),
                                  preferred_element_type=F32)
            att = jnp.where(causal, att, 0.0).astype(BF16)
            s = s_ref[h]
            o = _dot(att, v) + _dot((q * jnp.exp(b)).astype(BF16), s.astype(BF16))
            k_out = (k * jnp.exp(b_last - b)).astype(BF16)
            kv = lax.dot_general(k_out, v, (((0,), (0,)), ((), ())),
                                 preferred_element_type=F32)
            decay_col = jnp.transpose(
                jnp.broadcast_to(jnp.exp(b_last), (LANES, dk)))
            decay = jnp.concatenate([decay_col] * (dv // LANES), axis=1)
            s_ref[h] = decay * s + kv
            o_n = _rms(o, gn_ref[...])
            o_ref[rs, vs] = (o_n * _silu(r_ref[rs, vs].astype(F32))).astype(o_ref.dtype)

    @pl.when(t == pl.num_programs(1) - 1)
    def _():
        s_out_ref[0] = s_ref[...]


def _gla(proj, b_all, s0, gn, row0, n_streams, stream_len, *, col_q, col_k, col_v, col_r):
    _, heads, dk, dv = s0.shape
    kd, vd = heads * dk, heads * dv
    blocks = _pick(stream_len // GLA_BLOCK, (4, 2, 1))
    rows = blocks * GLA_BLOCK
    steps = stream_len // rows
    r0 = row0 // rows
    cq, ck, cv, cr = col_q // kd, col_k // kd, col_v // vd, col_r // vd
    kern = functools.partial(_gla_kernel, heads=heads, dk=dk, dv=dv, blocks=blocks)
    row = lambda s, t: r0 + s * steps + t
    return pl.pallas_call(
        kern,
        grid=(n_streams, steps),
        in_specs=[
            pl.BlockSpec((rows, kd), lambda s, t: (row(s, t), cq)),
            pl.BlockSpec((rows, kd), lambda s, t: (row(s, t), ck)),
            pl.BlockSpec((rows, vd), lambda s, t: (row(s, t), cv)),
            pl.BlockSpec((rows, vd), lambda s, t: (row(s, t), cr)),
            pl.BlockSpec((rows, kd), lambda s, t: (row(s, t), 0)),
            pl.BlockSpec((1, heads, dk, dv), lambda s, t: (s, 0, 0, 0)),
            pl.BlockSpec((1, dv), lambda s, t: (0, 0)),
        ],
        out_specs=[
            pl.BlockSpec((rows, vd), lambda s, t: (s * steps + t, 0)),
            pl.BlockSpec((1, heads, dk, dv), lambda s, t: (s, 0, 0, 0)),
        ],
        out_shape=[
            jax.ShapeDtypeStruct((n_streams * stream_len, vd), BF16),
            jax.ShapeDtypeStruct(s0.shape, F32),
        ],
        scratch_shapes=[pltpu.VMEM((heads, dk, dv), F32)],
        compiler_params=_params("arbitrary", "arbitrary"),
        name="gla_mixer",
    )(proj, proj, proj, proj, b_all, s0, gn)


def _mix_kernel(a_ref, b_ref, wa_ref, wb_ref, x_ref, o_ref):
    o_ref[...] = x_ref[...] + (_dot(a_ref[...], wa_ref[...]) + _dot(b_ref[...], wb_ref[...]))


def _mix(a_out, b_out, w_out, x):
    m, ka = a_out.shape
    kb = b_out.shape[1]
    n = w_out.shape[1]
    assert ka == kb, "head groups of different widths need separate weight specs"
    bm = _pick(m, (1024, 512, 256, 128, 64))
    bn = _pick(n, (512, 256, 128))
    return pl.pallas_call(
        _mix_kernel,
        grid=(m // bm, n // bn),
        in_specs=[
            pl.BlockSpec((bm, ka), lambda i, j: (i, 0)),
            pl.BlockSpec((bm, kb), lambda i, j: (i, 0)),
            pl.BlockSpec((ka, bn), lambda i, j: (0, j)),
            pl.BlockSpec((kb, bn), lambda i, j: (1, j)),
            pl.BlockSpec((bm, bn), lambda i, j: (i, j)),
        ],
        out_specs=pl.BlockSpec((bm, bn), lambda i, j: (i, j)),
        out_shape=jax.ShapeDtypeStruct((m, n), F32),
        compiler_params=_params("parallel", "parallel"),
        name="gemm_out_proj_residual",
    )(a_out, b_out, w_out, w_out, x)


def _rmsnorm_kernel(x_ref, g_ref, o_ref):
    o_ref[...] = _rms(x_ref[...], g_ref[...]).astype(o_ref.dtype)


def _rmsnorm(x, g, out_dtype, row0=0, n_rows=None):
    n, d = x.shape
    n_rows = n if n_rows is None else n_rows
    bm = _pick(math.gcd(n_rows, row0) if row0 else n_rows, (256, 128, 64))
    r0 = row0 // bm
    return pl.pallas_call(
        _rmsnorm_kernel,
        grid=(n_rows // bm,),
        in_specs=[
            pl.BlockSpec((bm, d), lambda i: (r0 + i, 0)),
            pl.BlockSpec((1, d), lambda i: (0, 0)),
        ],
        out_specs=pl.BlockSpec((bm, d), lambda i: (i, 0)),
        out_shape=jax.ShapeDtypeStruct((n_rows, d), out_dtype),
        compiler_params=_params("parallel"),
        name="rmsnorm",
    )(x, g)


def _ffn_up_kernel(h_ref, wg_ref, wu_ref, o_ref):
    h = h_ref[...]
    g = _dot(h, wg_ref[...])
    u = _dot(h, wu_ref[...])
    o_ref[...] = (_silu(g) * u).astype(o_ref.dtype)


def _ffn_up(h, wg, wu):
    m, k = h.shape
    f = wg.shape[1]
    bm = _pick(m, (1024, 512, 256, 128, 64))
    bn = _pick(f, (512, 256, 128))
    return pl.pallas_call(
        _ffn_up_kernel,
        grid=(m // bm, f // bn),
        in_specs=[
            pl.BlockSpec((bm, k), lambda i, j: (i, 0)),
            pl.BlockSpec((k, bn), lambda i, j: (0, j)),
            pl.BlockSpec((k, bn), lambda i, j: (0, j)),
        ],
        out_specs=pl.BlockSpec((bm, bn), lambda i, j: (i, j)),
        out_shape=jax.ShapeDtypeStruct((m, f), BF16),
        compiler_params=_params("parallel", "parallel"),
        name="ffn_gate_up",
    )(h, wg, wu)


def _ffn_down_kernel(h_ref, w_ref, x_ref, o_ref):
    p = _dot(h_ref[...], w_ref[...])

    @pl.when(pl.program_id(2) == 0)
    def _():
        o_ref[...] = x_ref[...] + p

    @pl.when(pl.program_id(2) != 0)
    def _():
        o_ref[...] += p


def _ffn_down(h1, wd, x1):
    m, f = h1.shape
    n = wd.shape[1]
    bm = _pick(m, (1024, 512, 256, 128, 64))
    bn = _pick(n, (1024, 512, 256, 128))
    bk = _pick(f, (2816, 2048, 1024, 512, 256, 128))
    return pl.pallas_call(
        _ffn_down_kernel,
        grid=(m // bm, n // bn, f // bk),
        in_specs=[
            pl.BlockSpec((bm, bk), lambda i, j, k: (i, k)),
            pl.BlockSpec((bk, bn), lambda i, j, k: (k, j)),
            pl.BlockSpec((bm, bn), lambda i, j, k: (i, j)),
        ],
        out_specs=pl.BlockSpec((bm, bn), lambda i, j, k: (i, j)),
        out_shape=jax.ShapeDtypeStruct((m, n), F32),
        compiler_params=_params("parallel", "parallel", "arbitrary"),
        name="ffn_down_residual",
    )(h1, wd, x1)


def kernel(x_prompt, x_sample, state_gla, g_mix, w_in, w_s, b_s, ln_g, ln_b, w_gate_up, b_gate,
           gla_norm_g, w_out, g_ffn, w_ffn_gate, w_ffn_up, w_ffn_down, g_final):
    depth = g_mix.shape[0]
    n_p, len_p, d_model = x_prompt.shape
    n_s, len_s, _ = x_sample.shape
    rows_p, rows_s = n_p * len_p, n_s * len_s
    _, _, heads_b, dk, dv = state_gla.shape
    a_width = ln_g.shape[-1]
    a_heads, sgu_chunk = w_s.shape[1], w_s.shape[2]
    key_dim, val_dim = heads_b * dk, heads_b * dv
    rank = w_gate_up.shape[1]
    d_ff = w_ffn_gate.shape[-1]
    main_cols = 2 * a_width + 2 * key_dim + 2 * val_dim
    col_q = 2 * a_width
    col_k = col_q + key_dim
    col_v = col_k + key_dim
    col_r = col_v + val_dim
    f_pad = _round_up(d_ff, 1024)
    chunk_p = min(sgu_chunk, len_p)
    chunk_s = min(sgu_chunk, len_s)

    x = jnp.concatenate([x_prompt.reshape(rows_p, d_model), x_sample.reshape(rows_s, d_model)], axis=0)
    zero_state = jnp.zeros((n_p, heads_b, dk, dv), state_gla.dtype)

    gla_p, gla_s, vn_s = [], [], []
    for d in range(depth):
        w_main = w_in[d][:, :main_cols].astype(BF16)
        w_lr = jnp.pad(w_in[d][:, main_cols:], ((0, 0), (0, LANES - rank))).astype(BF16)
        w_up = jnp.pad(w_gate_up[d], ((0, LANES - rank), (0, 0))).astype(BF16)
        w_o = w_out[d].astype(BF16)
        wg = jnp.pad(w_ffn_gate[d], ((0, 0), (0, f_pad - d_ff))).astype(BF16)
        wu = jnp.pad(w_ffn_up[d], ((0, 0), (0, f_pad - d_ff))).astype(BF16)
        wd = jnp.pad(w_ffn_down[d], ((0, f_pad - d_ff), (0, 0))).astype(BF16)
        head_dim = a_width // a_heads
        bias_full = jnp.repeat(b_s[d].T, head_dim, axis=1)
        row = lambda v: v.reshape(1, -1)

        h, b_all = _prep(x, row(g_mix[d]), w_lr, w_up, row(b_gate[d]))
        proj = _gemm(h, w_main, BF16)

        sgu_args = (row(ln_g[d]), row(ln_b[d]))
        (a_p,) = _sgu(proj, 0, rows_p, w_s[d][:, :chunk_p, :chunk_p], bias_full[:chunk_p],
                      *sgu_args, chunk=chunk_p, emit_vn=False)
        a_s, vn = _sgu(proj, rows_p, rows_s, w_s[d][:, :chunk_s, :chunk_s], bias_full[:chunk_s],
                       *sgu_args, chunk=chunk_s, emit_vn=True)
        cols = dict(col_q=col_q, col_k=col_k, col_v=col_v, col_r=col_r)
        gn = row(gla_norm_g[d])
        o_p, s_p = _gla(proj, b_all, zero_state, gn, 0, n_p, len_p, **cols)
        o_s, s_s = _gla(proj, b_all, state_gla[d], gn, rows_p, n_s, len_s, **cols)
        a_out = jnp.concatenate([a_p, a_s], axis=0)
        b_out = jnp.concatenate([o_p, o_s], axis=0)

        x1 = _mix(a_out, b_out, w_o, x)
        h2 = _rmsnorm(x1, row(g_ffn[d]), BF16)
        h1 = _ffn_up(h2, wg, wu)
        x = _ffn_down(h1, wd, x1)

        gla_p.append(s_p)
        gla_s.append(s_s)
        vn_s.append(vn.reshape(n_s, len_s, a_width))

    gf = g_final.reshape(1, -1)
    y_p = _rmsnorm(x, gf, F32, 0, rows_p).reshape(n_p, len_p, d_model)
    y_s = _rmsnorm(x, gf, F32, rows_p, rows_s).reshape(n_s, len_s, d_model)
    return (y_p, y_s, jnp.stack(gla_p), jnp.stack(gla_s), jnp.stack(vn_s))
```

```python
import functools
import math

import jax
import jax.numpy as jnp
from jax import lax
from jax.experimental import pallas as pl
from jax.experimental.pallas import tpu as pltpu

F32 = jnp.float32
BF16 = jnp.bfloat16

EPS = 1e-6
GLA_BLOCK = 64
GATE_TAU = 16.0
LANES = 128
V7X_VMEM_LIMIT_BYTES = 56 * 1024 * 1024


def _pick(n, candidates):
    for c in candidates:
        if n % c == 0:
            return c
    raise ValueError(f"no tile in {candidates} divides {n}")


def _params(*semantics):
    return pltpu.CompilerParams(dimension_semantics=semantics,
                                vmem_limit_bytes=V7X_VMEM_LIMIT_BYTES)


def _dot(a, b):
    return jnp.dot(a, b, preferred_element_type=F32)


def _gelu_tanh(x):
    c = math.sqrt(2.0 / math.pi)
    return 0.5 * x * (1.0 + jnp.tanh(c * (x + 0.044715 * (x * x * x))))


def _silu(x):
    return x * (1.0 / (1.0 + jnp.exp(-x)))


def _rms(x, g):
    return x * lax.rsqrt(jnp.mean(x * x, axis=-1, keepdims=True) + EPS) * g


def _prep_kernel(xp_ref, xs_ref, g_ref, wlr_ref, wup_ref, bg_ref, tri_ref, h_ref, b_ref,
                 *, p_tiles, sub):
    def body(x_ref):
        for c in range(x_ref.shape[0] // sub):
            rs = slice(c * sub, (c + 1) * sub)
            hb = _rms(x_ref[rs, :], g_ref[...]).astype(BF16)
            h_ref[rs, :] = hb
            g_lr = _dot(hb, wlr_ref[...])
            z = _dot(g_lr.astype(BF16), wup_ref[...]) + bg_ref[...]
            log_a = (jnp.minimum(z, 0.0) - jnp.log1p(jnp.exp(-jnp.abs(z)))) * (1.0 / GATE_TAU)
            hi = log_a.astype(BF16)
            r1 = log_a - hi.astype(F32)
            mid = r1.astype(BF16)
            lo = (r1 - mid.astype(F32)).astype(BF16)
            tri = tri_ref[...]
            b_ref[rs, :] = _dot(tri, hi) + _dot(tri, mid) + _dot(tri, lo)

    _on_prompt_or_sample(pl.program_id(0), p_tiles, body, (xp_ref,), (xs_ref,))


def _on_prompt_or_sample(i, p_tiles, body, prompt_refs, sample_refs):
    @pl.when(i < p_tiles)
    def _():
        body(*prompt_refs)

    @pl.when(i >= p_tiles)
    def _():
        body(*sample_refs)


def _split_specs(block, p_tiles, col_map=lambda *g: 0):
    prompt = pl.BlockSpec(block, lambda i, *g: (jnp.minimum(i, p_tiles - 1), col_map(*g)))
    sample = pl.BlockSpec(block, lambda i, *g: (jnp.maximum(i - p_tiles, 0), col_map(*g)))
    return [prompt, sample]


def _prep(x_p, x_s, g, w_lr, w_up, b_gate):
    (n_p, d), n_s = x_p.shape, x_s.shape[0]
    n = n_p + n_s
    key_dim = w_up.shape[1]
    bm = _pick(math.gcd(n_p, n_s), (256, 128, 64))
    sub = _pick(bm, (128, 64))
    idx = jnp.arange(sub)
    tri = ((idx[None, :] <= idx[:, None])
           & (idx[None, :] // GLA_BLOCK == idx[:, None] // GLA_BLOCK)).astype(BF16)
    p_tiles = n_p // bm
    const = lambda shape: pl.BlockSpec(shape, lambda i: (0, 0))
    return pl.pallas_call(
        functools.partial(_prep_kernel, p_tiles=p_tiles, sub=sub),
        grid=(n // bm,),
        in_specs=_split_specs((bm, d), p_tiles) + [
            const((1, d)), const(w_lr.shape), const(w_up.shape), const((1, key_dim)),
            const((sub, sub)),
        ],
        out_specs=[
            pl.BlockSpec((bm, d), lambda i: (i, 0)),
            pl.BlockSpec((bm, key_dim), lambda i: (i, 0)),
        ],
        out_shape=[
            jax.ShapeDtypeStruct((n, d), BF16),
            jax.ShapeDtypeStruct((n, key_dim), F32),
        ],
        compiler_params=_params("parallel"),
        name="prep_rmsnorm_gate",
    )(x_p, x_s, g, w_lr, w_up, b_gate, tri)


def _gemm_kernel(a_ref, w_ref, o_ref):
    o_ref[...] = _dot(a_ref[...], w_ref[...]).astype(o_ref.dtype)


def _gemm(a, w, out_dtype):
    m, k = a.shape
    n = w.shape[1]
    bm = _pick(m, (1024, 512, 256, 128, 64))
    bn = _pick(n, (1024, 512, 256, 128))
    return pl.pallas_call(
        _gemm_kernel,
        grid=(m // bm, n // bn),
        in_specs=[
            pl.BlockSpec((bm, k), lambda i, j: (i, 0)),
            pl.BlockSpec((k, bn), lambda i, j: (0, j)),
        ],
        out_specs=pl.BlockSpec((bm, bn), lambda i, j: (i, j)),
        out_shape=jax.ShapeDtypeStruct((m, n), out_dtype),
        compiler_params=_params("parallel", "parallel"),
        name="gemm_in_proj",
    )(a, w)


def _sgu_kernel(u_ref, v_ref, w_ref, bias_ref, lng_ref, lnb_ref, o_ref, *vn_refs,
                chunk, heads, head_dim):
    rows = u_ref.shape[0]
    n_chunks = rows // chunk
    v = _gelu_tanh(v_ref[...].astype(F32))
    mu = jnp.mean(v, axis=-1, keepdims=True)
    vc = v - mu
    var = jnp.mean(vc * vc, axis=-1, keepdims=True)
    vn = vc * lax.rsqrt(var + EPS) * lng_ref[...] + lnb_ref[...]
    if vn_refs:
        vn_refs[0][...] = vn
    vnb = vn.astype(BF16)
    ri = lax.broadcasted_iota(jnp.int32, (chunk, chunk), 0)
    ci = lax.broadcasted_iota(jnp.int32, (chunk, chunk), 1)
    causal = ci <= ri
    for h in range(heads):
        lo, hi = h * head_dim, (h + 1) * head_dim
        w = jnp.where(causal, w_ref[h], 0.0).astype(BF16)
        rhs = jnp.concatenate(
            [vnb[c * chunk:(c + 1) * chunk, lo:hi] for c in range(n_chunks)], axis=1)
        mixed = _dot(w, rhs)
        bias = bias_ref[:, lo:hi]
        for c in range(n_chunks):
            rs = slice(c * chunk, (c + 1) * chunk)
            u = _gelu_tanh(u_ref[rs, lo:hi].astype(F32))
            m = mixed[:, c * head_dim:(c + 1) * head_dim] + bias
            o_ref[rs, lo:hi] = (u * m).astype(o_ref.dtype)


def _sgu(proj, row0, n_rows, w_s, bias_full, ln_g, ln_b, *, chunk, emit_vn):
    heads = w_s.shape[0]
    width = ln_g.shape[1]
    head_dim = width // heads
    rows = _pick(n_rows, (2 * chunk, chunk))
    r0 = row0 // rows
    kern = functools.partial(_sgu_kernel, chunk=chunk, heads=heads, head_dim=head_dim)
    out_specs = [pl.BlockSpec((rows, width), lambda i: (i, 0))]
    out_shape = [jax.ShapeDtypeStruct((n_rows, width), BF16)]
    if emit_vn:
        out_specs.append(pl.BlockSpec((rows, width), lambda i: (i, 0)))
        out_shape.append(jax.ShapeDtypeStruct((n_rows, width), F32))
    return pl.pallas_call(
        kern,
        grid=(n_rows // rows,),
        in_specs=[
            pl.BlockSpec((rows, width), lambda i: (r0 + i, 0)),
            pl.BlockSpec((rows, width), lambda i: (r0 + i, 1)),
            pl.BlockSpec((heads, chunk, chunk), lambda i: (0, 0, 0)),
            pl.BlockSpec((chunk, width), lambda i: (0, 0)),
            pl.BlockSpec((1, width), lambda i: (0, 0)),
            pl.BlockSpec((1, width), lambda i: (0, 0)),
        ],
        out_specs=out_specs,
        out_shape=out_shape,
        compiler_params=_params("parallel"),
        name="sgu_mixer",
    )(proj, proj, w_s, bias_full, ln_g, ln_b)


def _gla_kernel(q_ref, k_ref, v_ref, r_ref, b_ref, s0_ref, gn_ref, o_ref, s_out_ref, s_ref,
                *, heads, dk, dv, blocks):
    t = pl.program_id(1)

    @pl.when(t == 0)
    def _():
        s_ref[...] = s0_ref[0]

    scale = dk ** -0.5
    ri = lax.broadcasted_iota(jnp.int32, (GLA_BLOCK, GLA_BLOCK), 0)
    ci = lax.broadcasted_iota(jnp.int32, (GLA_BLOCK, GLA_BLOCK), 1)
    causal = ci <= ri
    half = GLA_BLOCK // 2
    for blk in range(blocks):
        rs = slice(blk * GLA_BLOCK, (blk + 1) * GLA_BLOCK)
        for h in range(heads):
            ks = slice(h * dk, (h + 1) * dk)
            vs = slice(h * dv, (h + 1) * dv)
            q = q_ref[rs, ks].astype(F32) * scale
            k = k_ref[rs, ks].astype(F32)
            v = v_ref[rs, vs]
            b = b_ref[rs, ks]
            b_mid = b[half:half + 1, :]
            b_last = b[GLA_BLOCK - 1:GLA_BLOCK, :]
            q_in = (q * jnp.exp(b - b_mid)).astype(BF16)
            k_in = (k * jnp.exp(b_mid - b)).astype(BF16)
            att = lax.dot_general(q_in, k_in, (((1,), (1,)), ((), ())),
                                  preferred_element_type=F32)
            att = jnp.where(causal, att, 0.0).astype(BF16)
            s = s_ref[h]
            o = _dot(att, v) + _dot((q * jnp.exp(b)).astype(BF16), s.astype(BF16))
            k_out = (k * jnp.exp(b_last - b)).astype(BF16)
            kv = lax.dot_general(k_out, v, (((0,), (0,)), ((), ())),
                                 preferred_element_type=F32)
            decay_col = jnp.transpose(
                jnp.broadcast_to(jnp.exp(b_last), (LANES, dk)))
            decay = jnp.concatenate([decay_col] * (dv // LANES), axis=1)
            s_ref[h] = decay * s + kv
            o_n = _rms(o, gn_ref[...])
            o_ref[rs, vs] = (o_n * _silu(r_ref[rs, vs].astype(F32))).astype(o_ref.dtype)

    @pl.when(t == pl.num_programs(1) - 1)
    def _():
        s_out_ref[0] = s_ref[...]


def _gla(proj, b_all, s0, gn, row0, n_streams, stream_len, *, col_q, col_k, col_v, col_r):
    _, heads, dk, dv = s0.shape
    kd, vd = heads * dk, heads * dv
    blocks = _pick(stream_len // GLA_BLOCK, (4, 2, 1))
    rows = blocks * GLA_BLOCK
    steps = stream_len // rows
    r0 = row0 // rows
    cq, ck, cv, cr = col_q // kd, col_k // kd, col_v // vd, col_r // vd
    kern = functools.partial(_gla_kernel, heads=heads, dk=dk, dv=dv, blocks=blocks)
    row = lambda s, t: r0 + s * steps + t
    return pl.pallas_call(
        kern,
        grid=(n_streams, steps),
        in_specs=[
            pl.BlockSpec((rows, kd), lambda s, t: (row(s, t), cq)),
            pl.BlockSpec((rows, kd), lambda s, t: (row(s, t), ck)),
            pl.BlockSpec((rows, vd), lambda s, t: (row(s, t), cv)),
            pl.BlockSpec((rows, vd), lambda s, t: (row(s, t), cr)),
            pl.BlockSpec((rows, kd), lambda s, t: (row(s, t), 0)),
            pl.BlockSpec((1, heads, dk, dv), lambda s, t: (s, 0, 0, 0)),
            pl.BlockSpec((1, dv), lambda s, t: (0, 0)),
        ],
        out_specs=[
            pl.BlockSpec((rows, vd), lambda s, t: (s * steps + t, 0)),
            pl.BlockSpec((1, heads, dk, dv), lambda s, t: (s, 0, 0, 0)),
        ],
        out_shape=[
            jax.ShapeDtypeStruct((n_streams * stream_len, vd), BF16),
            jax.ShapeDtypeStruct(s0.shape, F32),
        ],
        scratch_shapes=[pltpu.VMEM((heads, dk, dv), F32)],
        compiler_params=_params("arbitrary", "arbitrary"),
        name="gla_mixer",
    )(proj, proj, proj, proj, b_all, s0, gn)


def _mix_kernel(ap_ref, as_ref, bp_ref, bs_ref, xp_ref, xs_ref, wa_ref, wb_ref, g_ref,
                x1_ref, xg_ref, ssq_ref, *, p_tiles):
    j = pl.program_id(1)

    def body(a_ref, b_ref, x_ref):
        x1 = x_ref[...] + (_dot(a_ref[...], wa_ref[...]) + _dot(b_ref[...], wb_ref[...]))
        x1_ref[...] = x1
        xg_ref[...] = (x1 * g_ref[...]).astype(xg_ref.dtype)
        sq = x1 * x1
        part = sq[:, :LANES]
        for c in range(1, sq.shape[1] // LANES):
            part = part + sq[:, c * LANES:(c + 1) * LANES]

        @pl.when(j == 0)
        def _():
            ssq_ref[...] = part

        @pl.when(j != 0)
        def _():
            ssq_ref[...] += part

    _on_prompt_or_sample(pl.program_id(0), p_tiles, body,
                         (ap_ref, bp_ref, xp_ref), (as_ref, bs_ref, xs_ref))


def _mix(a_p, a_s, b_p, b_s, x_p, x_s, w_out, g):
    (n_p, ka), n_s = a_p.shape, a_s.shape[0]
    m = n_p + n_s
    kb = b_p.shape[1]
    n = w_out.shape[1]
    assert ka == kb, "head groups of different widths need separate weight specs"
    bm = _pick(math.gcd(n_p, n_s), (512, 256, 128, 64))
    bn = _pick(n, (1024, 512, 256, 128))
    p_tiles = n_p // bm
    col0 = lambda j: 0
    colj = lambda j: j
    return pl.pallas_call(
        functools.partial(_mix_kernel, p_tiles=p_tiles),
        grid=(m // bm, n // bn),
        in_specs=(_split_specs((bm, ka), p_tiles, col0) + _split_specs((bm, kb), p_tiles, col0)
                  + _split_specs((bm, bn), p_tiles, colj) + [
            pl.BlockSpec((ka, bn), lambda i, j: (0, j)),
            pl.BlockSpec((kb, bn), lambda i, j: (1, j)),
            pl.BlockSpec((1, bn), lambda i, j: (0, j)),
        ]),
        out_specs=[
            pl.BlockSpec((bm, bn), lambda i, j: (i, j)),
            pl.BlockSpec((bm, bn), lambda i, j: (i, j)),
            pl.BlockSpec((bm, LANES), lambda i, j: (i, 0)),
        ],
        out_shape=[
            jax.ShapeDtypeStruct((m, n), F32),
            jax.ShapeDtypeStruct((m, n), BF16),
            jax.ShapeDtypeStruct((m, LANES), F32),
        ],
        compiler_params=_params("parallel", "arbitrary"),
        name="gemm_out_proj_residual",
    )(a_p, a_s, b_p, b_s, x_p, x_s, w_out, w_out, g)


def _rmsnorm_kernel(x_ref, g_ref, o_ref):
    o_ref[...] = _rms(x_ref[...], g_ref[...]).astype(o_ref.dtype)


def _rmsnorm(x, g, out_dtype, row0=0, n_rows=None):
    n, d = x.shape
    n_rows = n if n_rows is None else n_rows
    bm = _pick(math.gcd(n_rows, row0) if row0 else n_rows, (256, 128, 64))
    r0 = row0 // bm
    return pl.pallas_call(
        _rmsnorm_kernel,
        grid=(n_rows // bm,),
        in_specs=[
            pl.BlockSpec((bm, d), lambda i: (r0 + i, 0)),
            pl.BlockSpec((1, d), lambda i: (0, 0)),
        ],
        out_specs=pl.BlockSpec((bm, d), lambda i: (i, 0)),
        out_shape=jax.ShapeDtypeStruct((n_rows, d), out_dtype),
        compiler_params=_params("parallel"),
        name="rmsnorm",
    )(x, g)


def _ffn_up_kernel(xg_ref, ssq_ref, wg_ref, wu_ref, o_ref, *, d_model, d_ff):
    j = pl.program_id(1)
    bn = o_ref.shape[1]
    n_steps = -(-d_ff // bn)
    last = d_ff - (n_steps - 1) * bn
    rinv = lax.rsqrt(jnp.sum(ssq_ref[...], axis=-1, keepdims=True) * (1.0 / d_model) + EPS)

    def body(width):
        h = xg_ref[...]
        g = _dot(h, wg_ref[:, :width]) * rinv
        u = _dot(h, wu_ref[:, :width]) * rinv
        o_ref[:, :width] = (_silu(g) * u).astype(o_ref.dtype)

    if last == bn:
        body(bn)
    else:
        pl.when(j < n_steps - 1)(lambda: body(bn))
        pl.when(j == n_steps - 1)(lambda: body(last))


def _ffn_up(xg, ssq, wg, wu):
    m, k = xg.shape
    f = wg.shape[1]
    bm = _pick(m, (1024, 512, 256, 128, 64))
    bn = 512
    assert (f % bn) % 256 == 0, "the last column block must stay MXU-column aligned"
    return pl.pallas_call(
        functools.partial(_ffn_up_kernel, d_model=k, d_ff=f),
        grid=(m // bm, pl.cdiv(f, bn)),
        in_specs=[
            pl.BlockSpec((bm, k), lambda i, j: (i, 0)),
            pl.BlockSpec((bm, LANES), lambda i, j: (i, 0)),
            pl.BlockSpec((k, bn), lambda i, j: (0, j)),
            pl.BlockSpec((k, bn), lambda i, j: (0, j)),
        ],
        out_specs=pl.BlockSpec((bm, bn), lambda i, j: (i, j)),
        out_shape=jax.ShapeDtypeStruct((m, f), BF16),
        compiler_params=_params("parallel", "parallel"),
        name="ffn_gate_up",
    )(xg, ssq, wg, wu)


def _ffn_down_kernel(h_ref, w_ref, x_ref, o_ref, *, d_ff):
    k = pl.program_id(2)
    bk = h_ref.shape[1]
    n_steps = -(-d_ff // bk)
    last = d_ff - (n_steps - 1) * bk

    def body(width, first):
        p = _dot(h_ref[:, :width], w_ref[:width, :])
        if first:
            o_ref[...] = x_ref[...] + p
        else:
            o_ref[...] += p

    if n_steps == 1:
        body(last, True)
        return
    pl.when(k == 0)(lambda: body(bk, True))
    if n_steps > 2:
        pl.when((k > 0) & (k < n_steps - 1))(lambda: body(bk, False))
    pl.when(k == n_steps - 1)(lambda: body(last, False))


def _ffn_down(h1, wd, x1):
    m, f = h1.shape
    n = wd.shape[1]
    bm = _pick(m, (1024, 512, 256, 128, 64))
    bn = _pick(n, (1024, 512, 256, 128))
    bk = 2816
    assert (f % bk) % 256 == 0, "the last contraction block must stay MXU-row aligned"
    return pl.pallas_call(
        functools.partial(_ffn_down_kernel, d_ff=f),
        grid=(m // bm, n // bn, pl.cdiv(f, bk)),
        in_specs=[
            pl.BlockSpec((bm, bk), lambda i, j, k: (i, k)),
            pl.BlockSpec((bk, bn), lambda i, j, k: (k, j)),
            pl.BlockSpec((bm, bn), lambda i, j, k: (i, j)),
        ],
        out_specs=pl.BlockSpec((bm, bn), lambda i, j, k: (i, j)),
        out_shape=jax.ShapeDtypeStruct((m, n), F32),
        compiler_params=_params("parallel", "parallel", "arbitrary"),
        name="ffn_down_residual",
    )(h1, wd, x1)


def kernel(x_prompt, x_sample, state_gla, g_mix, w_in, w_s, b_s, ln_g, ln_b, w_gate_up, b_gate,
           gla_norm_g, w_out, g_ffn, w_ffn_gate, w_ffn_up, w_ffn_down, g_final):
    depth = g_mix.shape[0]
    n_p, len_p, d_model = x_prompt.shape
    n_s, len_s, _ = x_sample.shape
    rows_p, rows_s = n_p * len_p, n_s * len_s
    _, _, heads_b, dk, dv = state_gla.shape
    a_width = ln_g.shape[-1]
    a_heads, sgu_chunk = w_s.shape[1], w_s.shape[2]
    key_dim, val_dim = heads_b * dk, heads_b * dv
    rank = w_gate_up.shape[1]
    d_ff = w_ffn_gate.shape[-1]
    main_cols = 2 * a_width + 2 * key_dim + 2 * val_dim
    col_q = 2 * a_width
    col_k = col_q + key_dim
    col_v = col_k + key_dim
    col_r = col_v + val_dim
    chunk_p = min(sgu_chunk, len_p)
    chunk_s = min(sgu_chunk, len_s)

    x_p = x_prompt.reshape(rows_p, d_model)
    x_s = x_sample.reshape(rows_s, d_model)
    zero_state = jnp.zeros((n_p, heads_b, dk, dv), state_gla.dtype)

    gla_p, gla_s, vn_s = [], [], []
    for d in range(depth):
        w_main = w_in[d][:, :main_cols].astype(BF16)
        w_lr = jnp.pad(w_in[d][:, main_cols:], ((0, 0), (0, LANES - rank))).astype(BF16)
        w_up = jnp.pad(w_gate_up[d], ((0, LANES - rank), (0, 0))).astype(BF16)
        w_o = w_out[d].astype(BF16)
        wg = w_ffn_gate[d].astype(BF16)
        wu = w_ffn_up[d].astype(BF16)
        wd = w_ffn_down[d].astype(BF16)
        head_dim = a_width // a_heads
        bias_full = jnp.repeat(b_s[d].T, head_dim, axis=1)
        row = lambda v: v.reshape(1, -1)

        h, b_all = _prep(x_p, x_s, row(g_mix[d]), w_lr, w_up, row(b_gate[d]))
        proj = _gemm(h, w_main, BF16)

        sgu_args = (row(ln_g[d]), row(ln_b[d]))
        (a_p,) = _sgu(proj, 0, rows_p, w_s[d][:, :chunk_p, :chunk_p], bias_full[:chunk_p],
                      *sgu_args, chunk=chunk_p, emit_vn=False)
        a_s, vn = _sgu(proj, rows_p, rows_s, w_s[d][:, :chunk_s, :chunk_s], bias_full[:chunk_s],
                       *sgu_args, chunk=chunk_s, emit_vn=True)
        cols = dict(col_q=col_q, col_k=col_k, col_v=col_v, col_r=col_r)
        gn = row(gla_norm_g[d])
        o_p, s_p = _gla(proj, b_all, zero_state, gn, 0, n_p, len_p, **cols)
        o_s, s_s = _gla(proj, b_all, state_gla[d], gn, rows_p, n_s, len_s, **cols)

        x1, xg, ssq = _mix(a_p, a_s, o_p, o_s, x_p, x_s, w_o, row(g_ffn[d]))
        h1 = _ffn_up(xg, ssq, wg, wu)
        x = _ffn_down(h1, wd, x1)
        if d + 1 < depth:
            x_p, x_s = x[:rows_p], x[rows_p:]

        gla_p.append(s_p)
        gla_s.append(s_s)
        vn_s.append(vn.reshape(n_s, len_s, a_width))

    gf = g_final.reshape(1, -1)
    y_p = _rmsnorm(x, gf, F32, 0, rows_p).reshape(n_p, len_p, d_model)
    y_s = _rmsnorm(x, gf, F32, rows_p, rows_s).reshape(n_s, len_s, d_model)
    return (y_p, y_s, jnp.stack(gla_p), jnp.stack(gla_s), jnp.stack(vn_s))
```

```python
import functools
import math

import jax
import jax.numpy as jnp
from jax import lax
from jax.experimental import pallas as pl
from jax.experimental.pallas import tpu as pltpu

F32 = jnp.float32
BF16 = jnp.bfloat16

EPS = 1e-6
GLA_BLOCK = 64
GATE_TAU = 16.0
LANES = 128
V7X_VMEM_LIMIT_BYTES = 56 * 1024 * 1024


def _pick(n, candidates):
    for c in candidates:
        if n % c == 0:
            return c
    raise ValueError(f"no tile in {candidates} divides {n}")


def _params(*semantics):
    return pltpu.CompilerParams(dimension_semantics=semantics,
                                vmem_limit_bytes=V7X_VMEM_LIMIT_BYTES)


def _dot(a, b):
    return jnp.dot(a, b, preferred_element_type=F32)


def _gelu_tanh(x):
    c = math.sqrt(2.0 / math.pi)
    return 0.5 * x * (1.0 + jnp.tanh(c * (x + 0.044715 * (x * x * x))))


def _silu(x):
    return x * (1.0 / (1.0 + jnp.exp(-x)))


def _rms(x, g):
    return x * lax.rsqrt(jnp.mean(x * x, axis=-1, keepdims=True) + EPS) * g


def _prep_kernel(xp_ref, xs_ref, g_ref, wlr_ref, wup_ref, bg_ref, tri_ref, h_ref, b_ref,
                 *, p_tiles, sub):
    def body(x_ref):
        for c in range(x_ref.shape[0] // sub):
            rs = slice(c * sub, (c + 1) * sub)
            hb = _rms(x_ref[rs, :], g_ref[...]).astype(BF16)
            h_ref[rs, :] = hb
            g_lr = _dot(hb, wlr_ref[...])
            z = _dot(g_lr.astype(BF16), wup_ref[...]) + bg_ref[...]
            log_a = (jnp.minimum(z, 0.0) - jnp.log1p(jnp.exp(-jnp.abs(z)))) * (1.0 / GATE_TAU)
            hi = log_a.astype(BF16)
            r1 = log_a - hi.astype(F32)
            mid = r1.astype(BF16)
            lo = (r1 - mid.astype(F32)).astype(BF16)
            tri = tri_ref[...]
            b_ref[rs, :] = _dot(tri, hi) + _dot(tri, mid) + _dot(tri, lo)

    _on_prompt_or_sample(pl.program_id(0), p_tiles, body, (xp_ref,), (xs_ref,))


def _on_prompt_or_sample(i, p_tiles, body, prompt_refs, sample_refs):
    @pl.when(i < p_tiles)
    def _():
        body(*prompt_refs)

    @pl.when(i >= p_tiles)
    def _():
        body(*sample_refs)


def _split_specs(block, p_tiles, col_tiles=None, single_buffer_sample=False):
    def prompt_map(i, *g):
        col = jnp.where(i < p_tiles, g[0], col_tiles - 1) if col_tiles else 0
        return jnp.minimum(i, p_tiles - 1), col

    def sample_map(i, *g):
        col = jnp.where(i >= p_tiles, g[0], 0) if col_tiles else 0
        return jnp.maximum(i - p_tiles, 0), col

    mode = dict(pipeline_mode=pl.Buffered(1)) if single_buffer_sample else {}
    return [pl.BlockSpec(block, prompt_map), pl.BlockSpec(block, sample_map, **mode)]


def _prep(x_p, x_s, g, w_lr, w_up, b_gate):
    (n_p, d), n_s = x_p.shape, x_s.shape[0]
    n = n_p + n_s
    key_dim = w_up.shape[1]
    bm = _pick(math.gcd(n_p, n_s), (256, 128, 64))
    sub = _pick(bm, (128, 64))
    idx = jnp.arange(sub)
    tri = ((idx[None, :] <= idx[:, None])
           & (idx[None, :] // GLA_BLOCK == idx[:, None] // GLA_BLOCK)).astype(BF16)
    p_tiles = n_p // bm
    const = lambda shape: pl.BlockSpec(shape, lambda i: (0, 0))
    return pl.pallas_call(
        functools.partial(_prep_kernel, p_tiles=p_tiles, sub=sub),
        grid=(n // bm,),
        in_specs=_split_specs((bm, d), p_tiles) + [
            const((1, d)), const(w_lr.shape), const(w_up.shape), const((1, key_dim)),
            const((sub, sub)),
        ],
        out_specs=[
            pl.BlockSpec((bm, d), lambda i: (i, 0)),
            pl.BlockSpec((bm, key_dim), lambda i: (i, 0)),
        ],
        out_shape=[
            jax.ShapeDtypeStruct((n, d), BF16),
            jax.ShapeDtypeStruct((n, key_dim), F32),
        ],
        compiler_params=_params("parallel"),
        name="prep_rmsnorm_gate",
    )(x_p, x_s, g, w_lr, w_up, b_gate, tri)


def _in_proj_kernel(a_ref, w_ref, o_ref, *, epilogues):
    j = pl.program_id(1)
    for first, end, fn in epilogues:
        @pl.when((j >= first) & (j < end))
        def _(fn=fn):
            o_ref[...] = fn(_dot(a_ref[...], w_ref[...])).astype(o_ref.dtype)


def _in_proj(h, w, n_cols, col_epilogues):
    m, k = h.shape
    bm = _pick(m, (1024, 512, 256, 128, 64))
    bn = _pick(math.gcd(*[end for _, end, _ in col_epilogues]), (1024, 512, 256, 128))
    epilogues = tuple((first // bn, end // bn, fn) for first, end, fn in col_epilogues)
    return pl.pallas_call(
        functools.partial(_in_proj_kernel, epilogues=epilogues),
        grid=(m // bm, n_cols // bn),
        in_specs=[
            pl.BlockSpec((bm, k), lambda i, j: (i, 0)),
            pl.BlockSpec((k, bn), lambda i, j: (0, j)),
        ],
        out_specs=pl.BlockSpec((bm, bn), lambda i, j: (i, j)),
        out_shape=jax.ShapeDtypeStruct((m, n_cols), BF16),
        compiler_params=_params("parallel", "parallel"),
        name="gemm_in_proj",
    )(h, w)


def _sgu_kernel(u_ref, v_ref, w_ref, bias_ref, lng_ref, lnb_ref, o_ref, *vn_refs,
                chunk, heads, head_dim):
    rows = u_ref.shape[0]
    n_chunks = rows // chunk
    v = v_ref[...].astype(F32)
    mu = jnp.mean(v, axis=-1, keepdims=True)
    vc = v - mu
    var = jnp.mean(vc * vc, axis=-1, keepdims=True)
    vn = vc * lax.rsqrt(var + EPS) * lng_ref[...] + lnb_ref[...]
    if vn_refs:
        vn_refs[0][...] = vn
    vnb = vn.astype(BF16)
    ri = lax.broadcasted_iota(jnp.int32, (chunk, chunk), 0)
    ci = lax.broadcasted_iota(jnp.int32, (chunk, chunk), 1)
    causal = ci <= ri
    for h in range(heads):
        lo, hi = h * head_dim, (h + 1) * head_dim
        w = jnp.where(causal, w_ref[h], 0.0).astype(BF16)
        rhs = jnp.concatenate(
            [vnb[c * chunk:(c + 1) * chunk, lo:hi] for c in range(n_chunks)], axis=1)
        mixed = _dot(w, rhs)
        bias = bias_ref[:, lo:hi]
        for c in range(n_chunks):
            rs = slice(c * chunk, (c + 1) * chunk)
            u = u_ref[rs, lo:hi].astype(F32)
            m = mixed[:, c * head_dim:(c + 1) * head_dim] + bias
            o_ref[rs, lo:hi] = (u * m).astype(o_ref.dtype)


def _sgu(proj, row0, n_rows, w_s, bias_full, ln_g, ln_b, *, chunk, emit_vn):
    heads = w_s.shape[0]
    width = ln_g.shape[1]
    head_dim = width // heads
    rows = _pick(n_rows, (2 * chunk, chunk))
    r0 = row0 // rows
    kern = functools.partial(_sgu_kernel, chunk=chunk, heads=heads, head_dim=head_dim)
    out_specs = [pl.BlockSpec((rows, width), lambda i: (i, 0))]
    out_shape = [jax.ShapeDtypeStruct((n_rows, width), BF16)]
    if emit_vn:
        out_specs.append(pl.BlockSpec((rows, width), lambda i: (i, 0)))
        out_shape.append(jax.ShapeDtypeStruct((n_rows, width), F32))
    return pl.pallas_call(
        kern,
        grid=(n_rows // rows,),
        in_specs=[
            pl.BlockSpec((rows, width), lambda i: (r0 + i, 0)),
            pl.BlockSpec((rows, width), lambda i: (r0 + i, 1)),
            pl.BlockSpec((heads, chunk, chunk), lambda i: (0, 0, 0)),
            pl.BlockSpec((chunk, width), lambda i: (0, 0)),
            pl.BlockSpec((1, width), lambda i: (0, 0)),
            pl.BlockSpec((1, width), lambda i: (0, 0)),
        ],
        out_specs=out_specs,
        out_shape=out_shape,
        compiler_params=_params("parallel"),
        name="sgu_mixer",
    )(proj, proj, w_s, bias_full, ln_g, ln_b)


def _gla_kernel(q_ref, k_ref, v_ref, r_ref, b_ref, s0_ref, gn_ref, o_ref, s_out_ref, s_ref,
                *, heads, dk, dv, blocks):
    t = pl.program_id(1)

    @pl.when(t == 0)
    def _():
        s_ref[...] = s0_ref[0]

    ri = lax.broadcasted_iota(jnp.int32, (GLA_BLOCK, GLA_BLOCK), 0)
    ci = lax.broadcasted_iota(jnp.int32, (GLA_BLOCK, GLA_BLOCK), 1)
    causal = ci <= ri
    half = GLA_BLOCK // 2
    for blk in range(blocks):
        rs = slice(blk * GLA_BLOCK, (blk + 1) * GLA_BLOCK)
        for h in range(heads):
            ks = slice(h * dk, (h + 1) * dk)
            vs = slice(h * dv, (h + 1) * dv)
            q = q_ref[rs, ks].astype(F32)
            k = k_ref[rs, ks].astype(F32)
            v = v_ref[rs, vs]
            b = b_ref[rs, ks]
            b_mid = b[half:half + 1, :]
            b_last = b[GLA_BLOCK - 1:GLA_BLOCK, :]
            q_in = (q * jnp.exp(b - b_mid)).astype(BF16)
            k_in = (k * jnp.exp(b_mid - b)).astype(BF16)
            att = lax.dot_general(q_in, k_in, (((1,), (1,)), ((), ())),
                                  preferred_element_type=F32)
            att = jnp.where(causal, att, 0.0).astype(BF16)
            s = s_ref[h]
            o = _dot(att, v) + _dot((q * jnp.exp(b)).astype(BF16), s.astype(BF16))
            k_out = (k * jnp.exp(b_last - b)).astype(BF16)
            kv = lax.dot_general(k_out, v, (((0,), (0,)), ((), ())),
                                 preferred_element_type=F32)
            decay_col = jnp.transpose(
                jnp.broadcast_to(jnp.exp(b_last), (LANES, dk)))
            decay = jnp.concatenate([decay_col] * (dv // LANES), axis=1)
            s_ref[h] = decay * s + kv
            o_n = _rms(o, gn_ref[...])
            o_ref[rs, vs] = (o_n * r_ref[rs, vs].astype(F32)).astype(o_ref.dtype)

    @pl.when(t == pl.num_programs(1) - 1)
    def _():
        s_out_ref[0] = s_ref[...]


def _gla(proj, b_all, s0, gn, row0, n_streams, stream_len, *, col_q, col_k, col_v, col_r):
    _, heads, dk, dv = s0.shape
    kd, vd = heads * dk, heads * dv
    blocks = _pick(stream_len // GLA_BLOCK, (4, 2, 1))
    rows = blocks * GLA_BLOCK
    steps = stream_len // rows
    r0 = row0 // rows
    cq, ck, cv, cr = col_q // kd, col_k // kd, col_v // vd, col_r // vd
    kern = functools.partial(_gla_kernel, heads=heads, dk=dk, dv=dv, blocks=blocks)
    row = lambda s, t: r0 + s * steps + t
    return pl.pallas_call(
        kern,
        grid=(n_streams, steps),
        in_specs=[
            pl.BlockSpec((rows, kd), lambda s, t: (row(s, t), cq)),
            pl.BlockSpec((rows, kd), lambda s, t: (row(s, t), ck)),
            pl.BlockSpec((rows, vd), lambda s, t: (row(s, t), cv)),
            pl.BlockSpec((rows, vd), lambda s, t: (row(s, t), cr)),
            pl.BlockSpec((rows, kd), lambda s, t: (row(s, t), 0)),
            pl.BlockSpec((1, heads, dk, dv), lambda s, t: (s, 0, 0, 0)),
            pl.BlockSpec((1, dv), lambda s, t: (0, 0)),
        ],
        out_specs=[
            pl.BlockSpec((rows, vd), lambda s, t: (s * steps + t, 0)),
            pl.BlockSpec((1, heads, dk, dv), lambda s, t: (s, 0, 0, 0)),
        ],
        out_shape=[
            jax.ShapeDtypeStruct((n_streams * stream_len, vd), BF16),
            jax.ShapeDtypeStruct(s0.shape, F32),
        ],
        scratch_shapes=[pltpu.VMEM((heads, dk, dv), F32)],
        compiler_params=_params("arbitrary", "arbitrary"),
        name="gla_mixer",
    )(proj, proj, proj, proj, b_all, s0, gn)


def _mix_kernel(ap_ref, as_ref, bp_ref, bs_ref, xp_ref, xs_ref, wa_ref, wb_ref, g_ref,
                x1_ref, xg_ref, ssq_ref, *, p_tiles):
    j = pl.program_id(1)

    def body(a_ref, b_ref, x_ref):
        x1 = x_ref[...] + (_dot(a_ref[...], wa_ref[...]) + _dot(b_ref[...], wb_ref[...]))
        x1_ref[...] = x1
        xg_ref[...] = (x1 * g_ref[...]).astype(xg_ref.dtype)
        sq = x1 * x1
        part = sq[:, :LANES]
        for c in range(1, sq.shape[1] // LANES):
            part = part + sq[:, c * LANES:(c + 1) * LANES]

        @pl.when(j == 0)
        def _():
            ssq_ref[...] = part

        @pl.when(j != 0)
        def _():
            ssq_ref[...] += part

    _on_prompt_or_sample(pl.program_id(0), p_tiles, body,
                         (ap_ref, bp_ref, xp_ref), (as_ref, bs_ref, xs_ref))


def _mix(a_p, a_s, b_p, b_s, x_p, x_s, w_out, g):
    (n_p, ka), n_s = a_p.shape, a_s.shape[0]
    m = n_p + n_s
    kb = b_p.shape[1]
    n = w_out.shape[1]
    assert ka == kb, "head groups of different widths need separate weight specs"
    bm = _pick(math.gcd(n_p, n_s), (1024, 512, 256, 128, 64))
    bn = _pick(n, (512, 256, 128))
    p_tiles = n_p // bm
    split = functools.partial(_split_specs, p_tiles=p_tiles, single_buffer_sample=True)
    return pl.pallas_call(
        functools.partial(_mix_kernel, p_tiles=p_tiles),
        grid=(m // bm, n // bn),
        in_specs=(split((bm, ka)) + split((bm, kb)) + split((bm, bn), col_tiles=n // bn) + [
            pl.BlockSpec((ka, bn), lambda i, j: (0, j)),
            pl.BlockSpec((kb, bn), lambda i, j: (1, j)),
            pl.BlockSpec((1, bn), lambda i, j: (0, j)),
        ]),
        out_specs=[
            pl.BlockSpec((bm, bn), lambda i, j: (i, j)),
            pl.BlockSpec((bm, bn), lambda i, j: (i, j)),
            pl.BlockSpec((bm, LANES), lambda i, j: (i, 0)),
        ],
        out_shape=[
            jax.ShapeDtypeStruct((m, n), F32),
            jax.ShapeDtypeStruct((m, n), BF16),
            jax.ShapeDtypeStruct((m, LANES), F32),
        ],
        compiler_params=_params("parallel", "arbitrary"),
        name="gemm_out_proj_residual",
    )(a_p, a_s, b_p, b_s, x_p, x_s, w_out, w_out, g)


def _rmsnorm_kernel(x_ref, g_ref, o_ref):
    o_ref[...] = _rms(x_ref[...], g_ref[...]).astype(o_ref.dtype)


def _rmsnorm(x, g, out_dtype, row0=0, n_rows=None):
    n, d = x.shape
    n_rows = n if n_rows is None else n_rows
    bm = _pick(math.gcd(n_rows, row0) if row0 else n_rows, (256, 128, 64))
    r0 = row0 // bm
    return pl.pallas_call(
        _rmsnorm_kernel,
        grid=(n_rows // bm,),
        in_specs=[
            pl.BlockSpec((bm, d), lambda i: (r0 + i, 0)),
            pl.BlockSpec((1, d), lambda i: (0, 0)),
        ],
        out_specs=pl.BlockSpec((bm, d), lambda i: (i, 0)),
        out_shape=jax.ShapeDtypeStruct((n_rows, d), out_dtype),
        compiler_params=_params("parallel"),
        name="rmsnorm",
    )(x, g)


def _ffn_up_kernel(xg_ref, ssq_ref, wg_ref, wu_ref, o_ref, *, d_model, d_ff):
    j = pl.program_id(1)
    bn = o_ref.shape[1]
    n_steps = -(-d_ff // bn)
    last = d_ff - (n_steps - 1) * bn
    rinv = lax.rsqrt(jnp.sum(ssq_ref[...], axis=-1, keepdims=True) * (1.0 / d_model) + EPS)

    def body(width):
        h = xg_ref[...]
        g = _dot(h, wg_ref[:, :width]) * rinv
        u = _dot(h, wu_ref[:, :width]) * rinv
        o_ref[:, :width] = (_silu(g) * u).astype(o_ref.dtype)

    if last == bn:
        body(bn)
    else:
        pl.when(j == 0)(lambda: body(last))
        pl.when(j > 0)(lambda: body(bn))


def _ffn_up(xg, ssq, wg, wu):
    m, k = xg.shape
    f = wg.shape[1]
    bm = _pick(m, (1024, 512, 256, 128, 64))
    bn = 512
    assert (f % bn) % 256 == 0, "the edge column block must stay MXU-column aligned"
    n_steps = pl.cdiv(f, bn)
    col = (lambda j: j) if f % bn == 0 else (lambda j: jnp.where(j == 0, n_steps - 1, j - 1))
    return pl.pallas_call(
        functools.partial(_ffn_up_kernel, d_model=k, d_ff=f),
        grid=(m // bm, n_steps),
        in_specs=[
            pl.BlockSpec((bm, k), lambda i, j: (i, 0)),
            pl.BlockSpec((bm, LANES), lambda i, j: (i, 0)),
            pl.BlockSpec((k, bn), lambda i, j: (0, col(j))),
            pl.BlockSpec((k, bn), lambda i, j: (0, col(j))),
        ],
        out_specs=pl.BlockSpec((bm, bn), lambda i, j: (i, col(j))),
        out_shape=jax.ShapeDtypeStruct((m, f), BF16),
        compiler_params=_params("parallel", "parallel"),
        name="ffn_gate_up",
    )(xg, ssq, wg, wu)


def _ffn_down_kernel(h_ref, w_ref, x_ref, o_ref, *, d_ff):
    k = pl.program_id(2)
    bk = h_ref.shape[1]
    n_steps = -(-d_ff // bk)
    last = d_ff - (n_steps - 1) * bk

    def body(width, first):
        p = _dot(h_ref[:, :width], w_ref[:width, :])
        if first:
            o_ref[...] = x_ref[...] + p
        else:
            o_ref[...] += p

    if n_steps == 1:
        body(last, True)
        return
    pl.when(k == 0)(lambda: body(bk, True))
    if n_steps > 2:
        pl.when((k > 0) & (k < n_steps - 1))(lambda: body(bk, False))
    pl.when(k == n_steps - 1)(lambda: body(last, False))


def _ffn_down(h1, wd, x1):
    m, f = h1.shape
    n = wd.shape[1]
    bm = _pick(m, (1024, 512, 256, 128, 64))
    bn = _pick(n, (1024, 512, 256, 128))
    bk = 2816
    assert (f % bk) % 256 == 0, "the last contraction block must stay MXU-row aligned"
    return pl.pallas_call(
        functools.partial(_ffn_down_kernel, d_ff=f),
        grid=(m // bm, n // bn, pl.cdiv(f, bk)),
        in_specs=[
            pl.BlockSpec((bm, bk), lambda i, j, k: (i, k)),
            pl.BlockSpec((bk, bn), lambda i, j, k: (k, j)),
            pl.BlockSpec((bm, bn), lambda i, j, k: (i, j)),
        ],
        out_specs=pl.BlockSpec((bm, bn), lambda i, j, k: (i, j)),
        out_shape=jax.ShapeDtypeStruct((m, n), F32),
        compiler_params=_params("parallel", "parallel", "arbitrary"),
        name="ffn_down_residual",
    )(h1, wd, x1)


def kernel(x_prompt, x_sample, state_gla, g_mix, w_in, w_s, b_s, ln_g, ln_b, w_gate_up, b_gate,
           gla_norm_g, w_out, g_ffn, w_ffn_gate, w_ffn_up, w_ffn_down, g_final):
    depth = g_mix.shape[0]
    n_p, len_p, d_model = x_prompt.shape
    n_s, len_s, _ = x_sample.shape
    rows_p, rows_s = n_p * len_p, n_s * len_s
    _, _, heads_b, dk, dv = state_gla.shape
    a_width = ln_g.shape[-1]
    a_heads, sgu_chunk = w_s.shape[1], w_s.shape[2]
    key_dim, val_dim = heads_b * dk, heads_b * dv
    rank = w_gate_up.shape[1]
    d_ff = w_ffn_gate.shape[-1]
    main_cols = 2 * a_width + 2 * key_dim + 2 * val_dim
    col_q = 2 * a_width
    col_k = col_q + key_dim
    col_v = col_k + key_dim
    col_r = col_v + val_dim
    chunk_p = min(sgu_chunk, len_p)
    chunk_s = min(sgu_chunk, len_s)

    x_p = x_prompt.reshape(rows_p, d_model)
    x_s = x_sample.reshape(rows_s, d_model)
    zero_state = jnp.zeros((n_p, heads_b, dk, dv), state_gla.dtype)

    gla_p, gla_s, vn_s = [], [], []
    for d in range(depth):
        w_in_b = w_in[d].astype(BF16)
        w_lr = jnp.pad(w_in_b[:, main_cols:], ((0, 0), (0, LANES - rank)))
        w_up = jnp.pad(w_gate_up[d], ((0, LANES - rank), (0, 0))).astype(BF16)
        w_o = w_out[d].astype(BF16)
        wg = w_ffn_gate[d].astype(BF16)
        wu = w_ffn_up[d].astype(BF16)
        wd = w_ffn_down[d].astype(BF16)
        head_dim = a_width // a_heads
        bias_full = jnp.repeat(b_s[d].T, head_dim, axis=1)
        row = lambda v: v.reshape(1, -1)

        h, b_all = _prep(x_p, x_s, row(g_mix[d]), w_lr, w_up, row(b_gate[d]))
        q_scale = dk ** -0.5
        proj = _in_proj(h, w_in_b, main_cols, (
            (0, col_q, _gelu_tanh),
            (col_q, col_k, lambda t: t * q_scale),
            (col_k, col_r, lambda t: t),
            (col_r, main_cols, _silu),
        ))

        sgu_args = (row(ln_g[d]), row(ln_b[d]))
        (a_p,) = _sgu(proj, 0, rows_p, w_s[d][:, :chunk_p, :chunk_p], bias_full[:chunk_p],
                      *sgu_args, chunk=chunk_p, emit_vn=False)
        a_s, vn = _sgu(proj, rows_p, rows_s, w_s[d][:, :chunk_s, :chunk_s], bias_full[:chunk_s],
                       *sgu_args, chunk=chunk_s, emit_vn=True)
        cols = dict(col_q=col_q, col_k=col_k, col_v=col_v, col_r=col_r)
        gn = row(gla_norm_g[d])
        o_p, s_p = _gla(proj, b_all, zero_state, gn, 0, n_p, len_p, **cols)
        o_s, s_s = _gla(proj, b_all, state_gla[d], gn, rows_p, n_s, len_s, **cols)

        x1, xg, ssq = _mix(a_p, a_s, o_p, o_s, x_p, x_s, w_o, row(g_ffn[d]))
        h1 = _ffn_up(xg, ssq, wg, wu)
        x = _ffn_down(h1, wd, x1)
        if d + 1 < depth:
            x_p, x_s = x[:rows_p], x[rows_p:]

        gla_p.append(s_p)
        gla_s.append(s_s)
        vn_s.append(vn.reshape(n_s, len_s, a_width))

    gf = g_final.reshape(1, -1)
    y_p = _rmsnorm(x, gf, F32, 0, rows_p).reshape(n_p, len_p, d_model)
    y_s = _rmsnorm(x, gf, F32, rows_p, rows_s).reshape(n_s, len_s, d_model)
    return (y_p, y_s, jnp.stack(gla_p), jnp.stack(gla_s), jnp.stack(vn_s))
```

```python
import functools
import math

import jax
import jax.numpy as jnp
from jax import lax
from jax.experimental import pallas as pl
from jax.experimental.pallas import tpu as pltpu

F32 = jnp.float32
BF16 = jnp.bfloat16

EPS = 1e-6
GLA_BLOCK = 64
GATE_TAU = 16.0
LANES = 128
V7X_VMEM_LIMIT_BYTES = 56 * 1024 * 1024


def _pick(n, candidates):
    for c in candidates:
        if n % c == 0:
            return c
    raise ValueError(f"no tile in {candidates} divides {n}")


def _params(*semantics):
    return pltpu.CompilerParams(dimension_semantics=semantics,
                                vmem_limit_bytes=V7X_VMEM_LIMIT_BYTES)


def _dot(a, b):
    return jnp.dot(a, b, preferred_element_type=F32)


def _gelu_tanh(x):
    c = math.sqrt(2.0 / math.pi)
    return 0.5 * x * (1.0 + jnp.tanh(c * (x + 0.044715 * (x * x * x))))


def _silu(x):
    return x * (1.0 / (1.0 + jnp.exp(-x)))


def _rms(x, g):
    return x * lax.rsqrt(jnp.mean(x * x, axis=-1, keepdims=True) + EPS) * g


def _prep_kernel(xp_ref, xs_ref, g_ref, win_ref, wup_ref, bg_ref, tri_ref, h_ref, b_ref, wlr_ref,
                 *, p_tiles, sub, rank):
    @pl.when(pl.program_id(0) == 0)
    def _():
        pad = jnp.zeros((LANES - rank, win_ref.shape[1]), F32)
        wlr_ref[...] = jnp.transpose(jnp.concatenate([win_ref[...], pad], axis=0)).astype(BF16)

    def body(x_ref):
        for c in range(x_ref.shape[0] // sub):
            rs = slice(c * sub, (c + 1) * sub)
            hb = _rms(x_ref[rs, :], g_ref[...]).astype(BF16)
            h_ref[rs, :] = hb
            g_lr = _dot(hb, wlr_ref[...])
            z = _dot(g_lr.astype(BF16), wup_ref[...]) + bg_ref[...]
            log_a = (jnp.minimum(z, 0.0) - jnp.log1p(jnp.exp(-jnp.abs(z)))) * (1.0 / GATE_TAU)
            hi = log_a.astype(BF16)
            r1 = log_a - hi.astype(F32)
            mid = r1.astype(BF16)
            lo = (r1 - mid.astype(F32)).astype(BF16)
            tri = tri_ref[...]
            b_ref[rs, :] = _dot(tri, hi) + _dot(tri, mid) + _dot(tri, lo)

    _on_prompt_or_sample(pl.program_id(0), p_tiles, body, (xp_ref,), (xs_ref,))


def _on_prompt_or_sample(i, p_tiles, body, prompt_refs, sample_refs):
    @pl.when(i < p_tiles)
    def _():
        body(*prompt_refs)

    @pl.when(i >= p_tiles)
    def _():
        body(*sample_refs)


def _split_specs(block, p_tiles, col_tiles=None, single_buffer_sample=False):
    def prompt_map(i, *g):
        col = jnp.where(i < p_tiles, g[0], col_tiles - 1) if col_tiles else 0
        return jnp.minimum(i, p_tiles - 1), col

    def sample_map(i, *g):
        col = jnp.where(i >= p_tiles, g[0], 0) if col_tiles else 0
        return jnp.maximum(i - p_tiles, 0), col

    mode = dict(pipeline_mode=pl.Buffered(1)) if single_buffer_sample else {}
    return [pl.BlockSpec(block, prompt_map), pl.BlockSpec(block, sample_map, **mode)]


def _prep(x_p, x_s, g, w_in_t, gate_row0, rank, w_up, b_gate):
    (n_p, d), n_s = x_p.shape, x_s.shape[0]
    n = n_p + n_s
    key_dim = w_up.shape[1]
    assert gate_row0 % rank == 0 and rank % 8 == 0 and rank <= LANES
    bm = _pick(math.gcd(n_p, n_s), (256, 128, 64))
    sub = _pick(bm, (128, 64))
    idx = jnp.arange(sub)
    tri = ((idx[None, :] <= idx[:, None])
           & (idx[None, :] // GLA_BLOCK == idx[:, None] // GLA_BLOCK)).astype(BF16)
    p_tiles = n_p // bm
    const = lambda shape: pl.BlockSpec(shape, lambda i: (0, 0))
    return pl.pallas_call(
        functools.partial(_prep_kernel, p_tiles=p_tiles, sub=sub, rank=rank),
        grid=(n // bm,),
        in_specs=_split_specs((bm, d), p_tiles) + [
            const((1, d)),
            pl.BlockSpec((rank, d), lambda i: (gate_row0 // rank, 0)),
            const(w_up.shape), const((1, key_dim)), const((sub, sub)),
        ],
        out_specs=[
            pl.BlockSpec((bm, d), lambda i: (i, 0)),
            pl.BlockSpec((bm, key_dim), lambda i: (i, 0)),
        ],
        out_shape=[
            jax.ShapeDtypeStruct((n, d), BF16),
            jax.ShapeDtypeStruct((n, key_dim), F32),
        ],
        scratch_shapes=[pltpu.VMEM((d, LANES), BF16)],
        compiler_params=_params("arbitrary"),
        name="prep_rmsnorm_gate",
    )(x_p, x_s, g, w_in_t, w_up, b_gate, tri)


class _WeightStream:
    def __init__(self, w_hbms, wbuf, stage, sem, *, bn, n_col_steps, first_col, first_width,
                 next_col, transposed=False):
        self.w_hbms, self.wbuf, self.stage, self.sem = w_hbms, wbuf, stage, sem
        self.transposed = transposed
        self.bn = bn
        self.rows = stage.shape[2]
        self.tile_rows = wbuf.shape[2] - self.rows
        self.n_chunks = self.tile_rows // self.rows
        self.n_col_steps = n_col_steps
        self.first_col, self.first_width = first_col, first_width
        self.next_col = next_col
        self.j = pl.program_id(0)
        self.i = pl.program_id(1)

    def _copy(self, m, chunk, col0, width, slot):
        if self.transposed:
            assert width == self.bn
            src = self.w_hbms[m].at[pl.ds(col0 + chunk * self.rows, self.rows), :]
            dst = self.stage.at[m, slot]
        else:
            src = self.w_hbms[m].at[pl.ds(chunk * self.rows, self.rows), pl.ds(col0, width)]
            dst = self.stage.at[m, slot, :, pl.ds(0, width)]
        return pltpu.make_async_copy(src, dst, self.sem.at[m, slot])

    def _next_copy(self, m, chunk):
        col0 = pl.multiple_of(self.next_col(self.j) * self.bn, self.bn)
        return self._copy(m, chunk, col0, self.bn, chunk % 2)

    def advance(self):
        j, i, n_mats = self.j, self.i, len(self.w_hbms)

        @pl.when((j == 0) & (i == 0))
        def _():
            self.stage[...] = jnp.zeros_like(self.stage)
            for c in range(self.n_chunks):
                rs = slice(c * self.rows, (c + 1) * self.rows)
                for m in range(n_mats):
                    cp = self._copy(m, c, self.first_col * self.bn, self.first_width, c % 2)
                    cp.start()
                    cp.wait()
                    self.wbuf[m, 0, rs, :] = self.stage[m, c % 2].astype(BF16)

        has_next = j + 1 < self.n_col_steps

        @pl.when(has_next & (i >= 1) & (i <= self.n_chunks))
        def _():
            for m in range(n_mats):
                self._next_copy(m, i - 1).wait()

        @pl.when(has_next & (i < self.n_chunks))
        def _():
            for m in range(n_mats):
                self._next_copy(m, i).start()

    def cast_arrived_chunk(self):
        j, i = self.j, self.i
        valid = (j + 1 < self.n_col_steps) & (i >= 1) & (i <= self.n_chunks)
        row0 = pl.multiple_of(jnp.where(valid, (i - 1) * self.rows, self.tile_rows), self.rows)
        for m in range(len(self.w_hbms)):
            self.wbuf[m, (j + 1) % 2, pl.ds(row0, self.rows), :] = (
                self.stage[m, (i + 1) % 2].astype(BF16))

    def tile(self, m, width=None):
        w = self.wbuf[m, self.j % 2, :self.tile_rows, :]
        return w if width is None else w[:, :width]


def _weight_stream_scratch(n_mats, tile_shape, n_row_steps):
    assert n_row_steps >= 2, "weight chunks are prefetched across the row steps of one column tile"
    tile_rows, tile_cols = tile_shape
    n_chunks = 1
    while n_chunks * 2 <= min(n_row_steps - 1, 16) and tile_rows % (n_chunks * 2 * 16) == 0:
        n_chunks *= 2
    rows = tile_rows // n_chunks
    return [
        pltpu.VMEM((n_mats, 2, tile_rows + rows, tile_cols), BF16),
        pltpu.VMEM((n_mats, 2, rows, tile_cols), F32),
        pltpu.SemaphoreType.DMA((n_mats, 2)),
    ]


def _in_proj_kernel(a_ref, wt_hbm, o_ref, wbuf, stage, sem, *, epilogues, n_col_steps):
    bn = o_ref.shape[1]
    ws = _WeightStream([wt_hbm], wbuf, stage, sem, bn=bn, n_col_steps=n_col_steps,
                       first_col=0, first_width=bn, next_col=lambda j: j + 1, transposed=True)
    ws.advance()
    for first, end, fn in epilogues:
        @pl.when((ws.j >= first) & (ws.j < end))
        def _(fn=fn):
            ws.cast_arrived_chunk()
            acc = lax.dot_general(a_ref[...], ws.tile(0), (((1,), (1,)), ((), ())),
                                  preferred_element_type=F32)
            o_ref[...] = fn(acc).astype(o_ref.dtype)


def _in_proj(h, w_t, n_cols, col_epilogues):
    m, k = h.shape
    bm = _pick(m, (1024, 512, 256, 128, 64))
    bn = _pick(math.gcd(*[end for _, end, _ in col_epilogues]), (1024, 512, 256, 128))
    epilogues = tuple((first // bn, end // bn, fn) for first, end, fn in col_epilogues)
    n_col_steps = n_cols // bn
    return pl.pallas_call(
        functools.partial(_in_proj_kernel, epilogues=epilogues, n_col_steps=n_col_steps),
        grid=(n_col_steps, m // bm),
        in_specs=[
            pl.BlockSpec((bm, k), lambda j, i: (i, 0)),
            pl.BlockSpec(memory_space=pl.ANY),
        ],
        out_specs=pl.BlockSpec((bm, bn), lambda j, i: (i, j)),
        out_shape=jax.ShapeDtypeStruct((m, n_cols), BF16),
        scratch_shapes=_weight_stream_scratch(1, (bn, k), m // bm),
        compiler_params=_params("arbitrary", "arbitrary"),
        name="gemm_in_proj",
    )(h, w_t)


def _sgu_kernel(u_ref, v_ref, w_ref, bias_ref, lng_ref, lnb_ref, o_ref, *vn_refs,
                chunk, heads, head_dim):
    rows = u_ref.shape[0]
    n_chunks = rows // chunk
    v = v_ref[...].astype(F32)
    mu = jnp.mean(v, axis=-1, keepdims=True)
    vc = v - mu
    var = jnp.mean(vc * vc, axis=-1, keepdims=True)
    vn = vc * lax.rsqrt(var + EPS) * lng_ref[...] + lnb_ref[...]
    if vn_refs:
        vn_refs[0][...] = vn
    vnb = vn.astype(BF16)
    ri = lax.broadcasted_iota(jnp.int32, (chunk, chunk), 0)
    ci = lax.broadcasted_iota(jnp.int32, (chunk, chunk), 1)
    causal = ci <= ri
    for h in range(heads):
        lo, hi = h * head_dim, (h + 1) * head_dim
        w = jnp.where(causal, w_ref[h], 0.0).astype(BF16)
        rhs = jnp.concatenate(
            [vnb[c * chunk:(c + 1) * chunk, lo:hi] for c in range(n_chunks)], axis=1)
        mixed = _dot(w, rhs)
        bias = bias_ref[:, lo:hi]
        for c in range(n_chunks):
            rs = slice(c * chunk, (c + 1) * chunk)
            u = u_ref[rs, lo:hi].astype(F32)
            m = mixed[:, c * head_dim:(c + 1) * head_dim] + bias
            o_ref[rs, lo:hi] = (u * m).astype(o_ref.dtype)


def _sgu(proj, row0, n_rows, w_s, bias_full, ln_g, ln_b, *, chunk, emit_vn):
    heads = w_s.shape[0]
    width = ln_g.shape[1]
    head_dim = width // heads
    rows = _pick(n_rows, (2 * chunk, chunk))
    r0 = row0 // rows
    kern = functools.partial(_sgu_kernel, chunk=chunk, heads=heads, head_dim=head_dim)
    out_specs = [pl.BlockSpec((rows, width), lambda i: (i, 0))]
    out_shape = [jax.ShapeDtypeStruct((n_rows, width), BF16)]
    if emit_vn:
        out_specs.append(pl.BlockSpec((rows, width), lambda i: (i, 0)))
        out_shape.append(jax.ShapeDtypeStruct((n_rows, width), F32))
    return pl.pallas_call(
        kern,
        grid=(n_rows // rows,),
        in_specs=[
            pl.BlockSpec((rows, width), lambda i: (r0 + i, 0)),
            pl.BlockSpec((rows, width), lambda i: (r0 + i, 1)),
            pl.BlockSpec((heads, chunk, chunk), lambda i: (0, 0, 0)),
            pl.BlockSpec((chunk, width), lambda i: (0, 0)),
            pl.BlockSpec((1, width), lambda i: (0, 0)),
            pl.BlockSpec((1, width), lambda i: (0, 0)),
        ],
        out_specs=out_specs,
        out_shape=out_shape,
        compiler_params=_params("parallel"),
        name="sgu_mixer",
    )(proj, proj, w_s, bias_full, ln_g, ln_b)


def _gla_kernel(q_ref, k_ref, v_ref, r_ref, b_ref, s0_ref, gn_ref, o_ref, s_out_ref, s_ref,
                *, heads, dk, dv, blocks):
    t = pl.program_id(1)

    @pl.when(t == 0)
    def _():
        s_ref[...] = s0_ref[0]

    ri = lax.broadcasted_iota(jnp.int32, (GLA_BLOCK, GLA_BLOCK), 0)
    ci = lax.broadcasted_iota(jnp.int32, (GLA_BLOCK, GLA_BLOCK), 1)
    causal = ci <= ri
    half = GLA_BLOCK // 2
    for blk in range(blocks):
        rs = slice(blk * GLA_BLOCK, (blk + 1) * GLA_BLOCK)
        for h in range(heads):
            ks = slice(h * dk, (h + 1) * dk)
            vs = slice(h * dv, (h + 1) * dv)
            q = q_ref[rs, ks].astype(F32)
            k = k_ref[rs, ks].astype(F32)
            v = v_ref[rs, vs]
            b = b_ref[rs, ks]
            b_mid = b[half:half + 1, :]
            b_last = b[GLA_BLOCK - 1:GLA_BLOCK, :]
            q_in = (q * jnp.exp(b - b_mid)).astype(BF16)
            k_in = (k * jnp.exp(b_mid - b)).astype(BF16)
            att = lax.dot_general(q_in, k_in, (((1,), (1,)), ((), ())),
                                  preferred_element_type=F32)
            att = jnp.where(causal, att, 0.0).astype(BF16)
            s = s_ref[h]
            o = _dot(att, v) + _dot((q * jnp.exp(b)).astype(BF16), s.astype(BF16))
            k_out = (k * jnp.exp(b_last - b)).astype(BF16)
            kv = lax.dot_general(k_out, v, (((0,), (0,)), ((), ())),
                                 preferred_element_type=F32)
            decay_col = jnp.transpose(
                jnp.broadcast_to(jnp.exp(b_last), (LANES, dk)))
            decay = jnp.concatenate([decay_col] * (dv // LANES), axis=1)
            s_ref[h] = decay * s + kv
            o_n = _rms(o, gn_ref[...])
            o_ref[rs, vs] = (o_n * r_ref[rs, vs].astype(F32)).astype(o_ref.dtype)

    @pl.when(t == pl.num_programs(1) - 1)
    def _():
        s_out_ref[0] = s_ref[...]


def _gla(proj, b_all, s0, gn, row0, n_streams, stream_len, *, col_q, col_k, col_v, col_r):
    _, heads, dk, dv = s0.shape
    kd, vd = heads * dk, heads * dv
    blocks = _pick(stream_len // GLA_BLOCK, (4, 2, 1))
    rows = blocks * GLA_BLOCK
    steps = stream_len // rows
    r0 = row0 // rows
    cq, ck, cv, cr = col_q // kd, col_k // kd, col_v // vd, col_r // vd
    kern = functools.partial(_gla_kernel, heads=heads, dk=dk, dv=dv, blocks=blocks)
    row = lambda s, t: r0 + s * steps + t
    return pl.pallas_call(
        kern,
        grid=(n_streams, steps),
        in_specs=[
            pl.BlockSpec((rows, kd), lambda s, t: (row(s, t), cq)),
            pl.BlockSpec((rows, kd), lambda s, t: (row(s, t), ck)),
            pl.BlockSpec((rows, vd), lambda s, t: (row(s, t), cv)),
            pl.BlockSpec((rows, vd), lambda s, t: (row(s, t), cr)),
            pl.BlockSpec((rows, kd), lambda s, t: (row(s, t), 0)),
            pl.BlockSpec((1, heads, dk, dv), lambda s, t: (s, 0, 0, 0)),
            pl.BlockSpec((1, dv), lambda s, t: (0, 0)),
        ],
        out_specs=[
            pl.BlockSpec((rows, vd), lambda s, t: (s * steps + t, 0)),
            pl.BlockSpec((1, heads, dk, dv), lambda s, t: (s, 0, 0, 0)),
        ],
        out_shape=[
            jax.ShapeDtypeStruct((n_streams * stream_len, vd), BF16),
            jax.ShapeDtypeStruct(s0.shape, F32),
        ],
        scratch_shapes=[pltpu.VMEM((heads, dk, dv), F32)],
        compiler_params=_params("arbitrary", "arbitrary"),
        name="gla_mixer",
    )(proj, proj, proj, proj, b_all, s0, gn)


def _mix_kernel(ap_ref, as_ref, bp_ref, bs_ref, xp_ref, xs_ref, wa_ref, wb_ref, g_ref,
                x1_ref, xg_ref, ssq_ref, *, p_tiles):
    j = pl.program_id(1)

    def body(a_ref, b_ref, x_ref):
        x1 = x_ref[...] + (_dot(a_ref[...], wa_ref[...]) + _dot(b_ref[...], wb_ref[...]))
        x1_ref[...] = x1
        xg_ref[...] = (x1 * g_ref[...]).astype(xg_ref.dtype)
        sq = x1 * x1
        part = sq[:, :LANES]
        for c in range(1, sq.shape[1] // LANES):
            part = part + sq[:, c * LANES:(c + 1) * LANES]

        @pl.when(j == 0)
        def _():
            ssq_ref[...] = part

        @pl.when(j != 0)
        def _():
            ssq_ref[...] += part

    _on_prompt_or_sample(pl.program_id(0), p_tiles, body,
                         (ap_ref, bp_ref, xp_ref), (as_ref, bs_ref, xs_ref))


def _mix(a_p, a_s, b_p, b_s, x_p, x_s, w_out, g):
    (n_p, ka), n_s = a_p.shape, a_s.shape[0]
    m = n_p + n_s
    kb = b_p.shape[1]
    n = w_out.shape[1]
    assert ka == kb, "head groups of different widths need separate weight specs"
    bm = _pick(math.gcd(n_p, n_s), (1024, 512, 256, 128, 64))
    bn = _pick(n, (512, 256, 128))
    p_tiles = n_p // bm
    split = functools.partial(_split_specs, p_tiles=p_tiles, single_buffer_sample=True)
    return pl.pallas_call(
        functools.partial(_mix_kernel, p_tiles=p_tiles),
        grid=(m // bm, n // bn),
        in_specs=(split((bm, ka)) + split((bm, kb)) + split((bm, bn), col_tiles=n // bn) + [
            pl.BlockSpec((ka, bn), lambda i, j: (0, j)),
            pl.BlockSpec((kb, bn), lambda i, j: (1, j)),
            pl.BlockSpec((1, bn), lambda i, j: (0, j)),
        ]),
        out_specs=[
            pl.BlockSpec((bm, bn), lambda i, j: (i, j)),
            pl.BlockSpec((bm, bn), lambda i, j: (i, j)),
            pl.BlockSpec((bm, LANES), lambda i, j: (i, 0)),
        ],
        out_shape=[
            jax.ShapeDtypeStruct((m, n), F32),
            jax.ShapeDtypeStruct((m, n), BF16),
            jax.ShapeDtypeStruct((m, LANES), F32),
        ],
        compiler_params=_params("parallel", "arbitrary"),
        name="gemm_out_proj_residual",
    )(a_p, a_s, b_p, b_s, x_p, x_s, w_out, w_out, g)


def _rmsnorm_kernel(x_ref, g_ref, o_ref):
    o_ref[...] = _rms(x_ref[...], g_ref[...]).astype(o_ref.dtype)


def _rmsnorm(x, g, out_dtype, row0=0, n_rows=None):
    n, d = x.shape
    n_rows = n if n_rows is None else n_rows
    bm = _pick(math.gcd(n_rows, row0) if row0 else n_rows, (256, 128, 64))
    r0 = row0 // bm
    return pl.pallas_call(
        _rmsnorm_kernel,
        grid=(n_rows // bm,),
        in_specs=[
            pl.BlockSpec((bm, d), lambda i: (r0 + i, 0)),
            pl.BlockSpec((1, d), lambda i: (0, 0)),
        ],
        out_specs=pl.BlockSpec((bm, d), lambda i: (i, 0)),
        out_shape=jax.ShapeDtypeStruct((n_rows, d), out_dtype),
        compiler_params=_params("parallel"),
        name="rmsnorm",
    )(x, g)


def _ffn_up_kernel(xg_ref, ssq_ref, wg_hbm, wu_hbm, o_ref, wbuf, stage, sem, *, d_model, d_ff):
    bn = o_ref.shape[1]
    n_steps = -(-d_ff // bn)
    edge = d_ff - (n_steps - 1) * bn
    edge_first = edge != bn
    ws = _WeightStream([wg_hbm, wu_hbm], wbuf, stage, sem, bn=bn, n_col_steps=n_steps,
                       first_col=n_steps - 1 if edge_first else 0, first_width=edge,
                       next_col=(lambda j: j) if edge_first else (lambda j: j + 1))
    ws.advance()
    rinv = lax.rsqrt(jnp.sum(ssq_ref[...], axis=-1, keepdims=True) * (1.0 / d_model) + EPS)

    def body(width):
        ws.cast_arrived_chunk()
        h = xg_ref[...]
        g = _dot(h, ws.tile(0, width)) * rinv
        u = _dot(h, ws.tile(1, width)) * rinv
        o_ref[:, :width] = (_silu(g) * u).astype(o_ref.dtype)

    if edge_first:
        pl.when(ws.j == 0)(lambda: body(edge))
        pl.when(ws.j > 0)(lambda: body(bn))
    else:
        body(bn)


def _ffn_up(xg, ssq, wg, wu):
    m, k = xg.shape
    f = wg.shape[1]
    bm = _pick(m, (1024, 512, 256, 128, 64))
    bn = 512
    assert (f % bn) % 256 == 0, "the edge column block must stay MXU-column aligned"
    n_steps = pl.cdiv(f, bn)
    col = (lambda j: j) if f % bn == 0 else (lambda j: jnp.where(j == 0, n_steps - 1, j - 1))
    return pl.pallas_call(
        functools.partial(_ffn_up_kernel, d_model=k, d_ff=f),
        grid=(n_steps, m // bm),
        in_specs=[
            pl.BlockSpec((bm, k), lambda j, i: (i, 0)),
            pl.BlockSpec((bm, LANES), lambda j, i: (i, 0)),
            pl.BlockSpec(memory_space=pl.ANY),
            pl.BlockSpec(memory_space=pl.ANY),
        ],
        out_specs=pl.BlockSpec((bm, bn), lambda j, i: (i, col(j))),
        out_shape=jax.ShapeDtypeStruct((m, f), BF16),
        scratch_shapes=_weight_stream_scratch(2, (k, bn), m // bm),
        compiler_params=_params("arbitrary", "arbitrary"),
        name="ffn_gate_up",
    )(xg, ssq, wg, wu)


def _ffn_down_kernel(h_ref, w_ref, x_ref, o_ref, *, d_ff):
    k = pl.program_id(2)
    bk = h_ref.shape[1]
    n_steps = -(-d_ff // bk)
    last = d_ff - (n_steps - 1) * bk

    def body(width, first):
        p = _dot(h_ref[:, :width], w_ref[:width, :])
        if first:
            o_ref[...] = x_ref[...] + p
        else:
            o_ref[...] += p

    if n_steps == 1:
        body(last, True)
        return
    pl.when(k == 0)(lambda: body(bk, True))
    if n_steps > 2:
        pl.when((k > 0) & (k < n_steps - 1))(lambda: body(bk, False))
    pl.when(k == n_steps - 1)(lambda: body(last, False))


def _ffn_down(h1, wd, x1):
    m, f = h1.shape
    n = wd.shape[1]
    bm = _pick(m, (1024, 512, 256, 128, 64))
    bn = _pick(n, (1024, 512, 256, 128))
    bk = 2816
    assert (f % bk) % 256 == 0, "the last contraction block must stay MXU-row aligned"
    return pl.pallas_call(
        functools.partial(_ffn_down_kernel, d_ff=f),
        grid=(m // bm, n // bn, pl.cdiv(f, bk)),
        in_specs=[
            pl.BlockSpec((bm, bk), lambda i, j, k: (i, k)),
            pl.BlockSpec((bk, bn), lambda i, j, k: (k, j)),
            pl.BlockSpec((bm, bn), lambda i, j, k: (i, j)),
        ],
        out_specs=pl.BlockSpec((bm, bn), lambda i, j, k: (i, j)),
        out_shape=jax.ShapeDtypeStruct((m, n), F32),
        compiler_params=_params("parallel", "parallel", "arbitrary"),
        name="ffn_down_residual",
    )(h1, wd, x1)


def kernel(x_prompt, x_sample, state_gla, g_mix, w_in, w_s, b_s, ln_g, ln_b, w_gate_up, b_gate,
           gla_norm_g, w_out, g_ffn, w_ffn_gate, w_ffn_up, w_ffn_down, g_final):
    depth = g_mix.shape[0]
    n_p, len_p, d_model = x_prompt.shape
    n_s, len_s, _ = x_sample.shape
    rows_p, rows_s = n_p * len_p, n_s * len_s
    _, _, heads_b, dk, dv = state_gla.shape
    a_width = ln_g.shape[-1]
    a_heads, sgu_chunk = w_s.shape[1], w_s.shape[2]
    key_dim, val_dim = heads_b * dk, heads_b * dv
    rank = w_gate_up.shape[1]
    d_ff = w_ffn_gate.shape[-1]
    main_cols = 2 * a_width + 2 * key_dim + 2 * val_dim
    col_q = 2 * a_width
    col_k = col_q + key_dim
    col_v = col_k + key_dim
    col_r = col_v + val_dim
    chunk_p = min(sgu_chunk, len_p)
    chunk_s = min(sgu_chunk, len_s)

    x_p = x_prompt.reshape(rows_p, d_model)
    x_s = x_sample.reshape(rows_s, d_model)
    zero_state = jnp.zeros((n_p, heads_b, dk, dv), state_gla.dtype)

    gla_p, gla_s, vn_s = [], [], []
    for d in range(depth):
        w_up = jnp.pad(w_gate_up[d], ((0, LANES - rank), (0, 0))).astype(BF16)
        w_o = w_out[d].astype(BF16)
        wd = w_ffn_down[d].astype(BF16)
        head_dim = a_width // a_heads
        bias_full = jnp.repeat(b_s[d].T, head_dim, axis=1)
        row = lambda v: v.reshape(1, -1)

        w_in_t = w_in[d].T
        h, b_all = _prep(x_p, x_s, row(g_mix[d]), w_in_t, main_cols, rank, w_up, row(b_gate[d]))
        q_scale = dk ** -0.5
        proj = _in_proj(h, w_in_t, main_cols, (
            (0, col_q, _gelu_tanh),
            (col_q, col_k, lambda t: t * q_scale),
            (col_k, col_r, lambda t: t),
            (col_r, main_cols, _silu),
        ))

        sgu_args = (row(ln_g[d]), row(ln_b[d]))
        (a_p,) = _sgu(proj, 0, rows_p, w_s[d][:, :chunk_p, :chunk_p], bias_full[:chunk_p],
                      *sgu_args, chunk=chunk_p, emit_vn=False)
        a_s, vn = _sgu(proj, rows_p, rows_s, w_s[d][:, :chunk_s, :chunk_s], bias_full[:chunk_s],
                       *sgu_args, chunk=chunk_s, emit_vn=True)
        cols = dict(col_q=col_q, col_k=col_k, col_v=col_v, col_r=col_r)
        gn = row(gla_norm_g[d])
        o_p, s_p = _gla(proj, b_all, zero_state, gn, 0, n_p, len_p, **cols)
        o_s, s_s = _gla(proj, b_all, state_gla[d], gn, rows_p, n_s, len_s, **cols)

        x1, xg, ssq = _mix(a_p, a_s, o_p, o_s, x_p, x_s, w_o, row(g_ffn[d]))
        h1 = _ffn_up(xg, ssq, w_ffn_gate[d], w_ffn_up[d])
        x = _ffn_down(h1, wd, x1)
        if d + 1 < depth:
            x_p, x_s = x[:rows_p], x[rows_p:]

        gla_p.append(s_p)
        gla_s.append(s_s)
        vn_s.append(vn.reshape(n_s, len_s, a_width))

    gf = g_final.reshape(1, -1)
    y_p = _rmsnorm(x, gf, F32, 0, rows_p).reshape(n_p, len_p, d_model)
    y_s = _rmsnorm(x, gf, F32, rows_p, rows_s).reshape(n_s, len_s, d_model)
    return (y_p, y_s, jnp.stack(gla_p), jnp.stack(gla_s), jnp.stack(vn_s))
```

```python
import functools
import math

import jax
import jax.numpy as jnp
from jax import lax
from jax.experimental import pallas as pl
from jax.experimental.pallas import tpu as pltpu

F32 = jnp.float32
BF16 = jnp.bfloat16

EPS = 1e-6
GLA_BLOCK = 64
GATE_TAU = 16.0
LANES = 128
V7X_VMEM_LIMIT_BYTES = 56 * 1024 * 1024


def _pick(n, candidates):
    for c in candidates:
        if n % c == 0:
            return c
    raise ValueError(f"no tile in {candidates} divides {n}")


def _params(*semantics):
    return pltpu.CompilerParams(dimension_semantics=semantics,
                                vmem_limit_bytes=V7X_VMEM_LIMIT_BYTES)


def _dot(a, b):
    return jnp.dot(a, b, preferred_element_type=F32)


def _gelu_tanh(x):
    c = math.sqrt(2.0 / math.pi)
    return 0.5 * x * (1.0 + jnp.tanh(c * (x + 0.044715 * (x * x * x))))


def _silu(x):
    return x * (1.0 / (1.0 + jnp.exp(-x)))


def _rms(x, g):
    return x * lax.rsqrt(jnp.mean(x * x, axis=-1, keepdims=True) + EPS) * g


def _prep_kernel(xp_ref, xs_ref, g_ref, win_ref, wup_ref, bg_ref, tri_ref, h_ref, b_ref, wlr_ref,
                 *, p_tiles, sub, rank):
    @pl.when(pl.program_id(0) == 0)
    def _():
        pad = jnp.zeros((LANES - rank, win_ref.shape[1]), F32)
        wlr_ref[...] = jnp.transpose(jnp.concatenate([win_ref[...], pad], axis=0)).astype(BF16)

    def body(x_ref):
        for c in range(x_ref.shape[0] // sub):
            rs = slice(c * sub, (c + 1) * sub)
            hb = _rms(x_ref[rs, :], g_ref[...]).astype(BF16)
            h_ref[rs, :] = hb
            g_lr = _dot(hb, wlr_ref[...])
            z = _dot(g_lr.astype(BF16), wup_ref[...]) + bg_ref[...]
            log_a = (jnp.minimum(z, 0.0) - jnp.log1p(jnp.exp(-jnp.abs(z)))) * (1.0 / GATE_TAU)
            hi = log_a.astype(BF16)
            r1 = log_a - hi.astype(F32)
            mid = r1.astype(BF16)
            lo = (r1 - mid.astype(F32)).astype(BF16)
            tri = tri_ref[...]
            b_ref[rs, :] = _dot(tri, hi) + _dot(tri, mid) + _dot(tri, lo)

    _on_prompt_or_sample(pl.program_id(0), p_tiles, body, (xp_ref,), (xs_ref,))


def _on_prompt_or_sample(i, p_tiles, body, prompt_refs, sample_refs):
    @pl.when(i < p_tiles)
    def _():
        body(*prompt_refs)

    @pl.when(i >= p_tiles)
    def _():
        body(*sample_refs)


def _split_specs(block, p_tiles, col_tiles=None, single_buffer_sample=False):
    def prompt_map(i, *g):
        col = jnp.where(i < p_tiles, g[0], col_tiles - 1) if col_tiles else 0
        return jnp.minimum(i, p_tiles - 1), col

    def sample_map(i, *g):
        col = jnp.where(i >= p_tiles, g[0], 0) if col_tiles else 0
        return jnp.maximum(i - p_tiles, 0), col

    mode = dict(pipeline_mode=pl.Buffered(1)) if single_buffer_sample else {}
    return [pl.BlockSpec(block, prompt_map), pl.BlockSpec(block, sample_map, **mode)]


def _prep(x_p, x_s, g, w_in_t, gate_row0, rank, w_up, b_gate):
    (n_p, d), n_s = x_p.shape, x_s.shape[0]
    n = n_p + n_s
    key_dim = w_up.shape[1]
    assert gate_row0 % rank == 0 and rank % 8 == 0 and rank <= LANES
    bm = _pick(math.gcd(n_p, n_s), (512, 256, 128, 64))
    sub = _pick(bm, (128, 64))
    idx = jnp.arange(sub)
    tri = ((idx[None, :] <= idx[:, None])
           & (idx[None, :] // GLA_BLOCK == idx[:, None] // GLA_BLOCK)).astype(BF16)
    p_tiles = n_p // bm
    const = lambda shape: pl.BlockSpec(shape, lambda i: (0, 0))
    return pl.pallas_call(
        functools.partial(_prep_kernel, p_tiles=p_tiles, sub=sub, rank=rank),
        grid=(n // bm,),
        in_specs=_split_specs((bm, d), p_tiles, single_buffer_sample=True) + [
            const((1, d)),
            pl.BlockSpec((rank, d), lambda i: (gate_row0 // rank, 0)),
            const(w_up.shape), const((1, key_dim)), const((sub, sub)),
        ],
        out_specs=[
            pl.BlockSpec((bm, d), lambda i: (i, 0)),
            pl.BlockSpec((bm, key_dim), lambda i: (i, 0)),
        ],
        out_shape=[
            jax.ShapeDtypeStruct((n, d), BF16),
            jax.ShapeDtypeStruct((n, key_dim), F32),
        ],
        scratch_shapes=[pltpu.VMEM((d, LANES), BF16)],
        compiler_params=_params("arbitrary"),
        name="prep_rmsnorm_gate",
    )(x_p, x_s, g, w_in_t, w_up, b_gate, tri)


class _WeightStream:
    def __init__(self, w_hbms, wbuf, stage, sem, *, bn, n_col_steps, first_col, first_width,
                 next_col, transposed=False):
        self.w_hbms, self.wbuf, self.stage, self.sem = w_hbms, wbuf, stage, sem
        self.transposed = transposed
        self.bn = bn
        self.rows = stage.shape[2]
        self.tile_rows = wbuf.shape[2] - self.rows
        self.n_chunks = self.tile_rows // self.rows
        self.n_col_steps = n_col_steps
        self.first_col, self.first_width = first_col, first_width
        self.next_col = next_col
        self.j = pl.program_id(0)
        self.i = pl.program_id(1)

    def _copy(self, m, chunk, col0, width, slot):
        if self.transposed:
            assert width == self.bn
            src = self.w_hbms[m].at[pl.ds(col0 + chunk * self.rows, self.rows), :]
            dst = self.stage.at[m, slot]
        else:
            src = self.w_hbms[m].at[pl.ds(chunk * self.rows, self.rows), pl.ds(col0, width)]
            dst = self.stage.at[m, slot, :, pl.ds(0, width)]
        return pltpu.make_async_copy(src, dst, self.sem.at[m, slot])

    def _next_copy(self, m, chunk):
        col0 = pl.multiple_of(self.next_col(self.j) * self.bn, self.bn)
        return self._copy(m, chunk, col0, self.bn, chunk % 2)

    def advance(self):
        j, i, n_mats = self.j, self.i, len(self.w_hbms)

        @pl.when((j == 0) & (i == 0))
        def _():
            self.stage[...] = jnp.zeros_like(self.stage)
            for c in range(self.n_chunks):
                rs = slice(c * self.rows, (c + 1) * self.rows)
                for m in range(n_mats):
                    cp = self._copy(m, c, self.first_col * self.bn, self.first_width, c % 2)
                    cp.start()
                    cp.wait()
                    self.wbuf[m, 0, rs, :] = self.stage[m, c % 2].astype(BF16)

        has_next = j + 1 < self.n_col_steps

        @pl.when(has_next & (i >= 1) & (i <= self.n_chunks))
        def _():
            for m in range(n_mats):
                self._next_copy(m, i - 1).wait()

        @pl.when(has_next & (i < self.n_chunks))
        def _():
            for m in range(n_mats):
                self._next_copy(m, i).start()

    def cast_arrived_chunk(self):
        j, i = self.j, self.i
        valid = (j + 1 < self.n_col_steps) & (i >= 1) & (i <= self.n_chunks)
        row0 = pl.multiple_of(jnp.where(valid, (i - 1) * self.rows, self.tile_rows), self.rows)
        for m in range(len(self.w_hbms)):
            self.wbuf[m, (j + 1) % 2, pl.ds(row0, self.rows), :] = (
                self.stage[m, (i + 1) % 2].astype(BF16))

    def tile(self, m, width=None):
        w = self.wbuf[m, self.j % 2, :self.tile_rows, :]
        return w if width is None else w[:, :width]


def _weight_stream_scratch(n_mats, tile_shape, n_row_steps):
    assert n_row_steps >= 2, "weight chunks are prefetched across the row steps of one column tile"
    tile_rows, tile_cols = tile_shape
    n_chunks = 1
    while n_chunks * 2 <= min(n_row_steps - 1, 16) and tile_rows % (n_chunks * 2 * 16) == 0:
        n_chunks *= 2
    rows = tile_rows // n_chunks
    return [
        pltpu.VMEM((n_mats, 2, tile_rows + rows, tile_cols), BF16),
        pltpu.VMEM((n_mats, 2, rows, tile_cols), F32),
        pltpu.SemaphoreType.DMA((n_mats, 2)),
    ]


def _in_proj_kernel(a_ref, wt_hbm, o_ref, wbuf, stage, sem, *, epilogues, n_col_steps):
    bn = o_ref.shape[1]
    ws = _WeightStream([wt_hbm], wbuf, stage, sem, bn=bn, n_col_steps=n_col_steps,
                       first_col=0, first_width=bn, next_col=lambda j: j + 1, transposed=True)
    ws.advance()
    for first, end, fn in epilogues:
        @pl.when((ws.j >= first) & (ws.j < end))
        def _(fn=fn):
            ws.cast_arrived_chunk()
            acc = lax.dot_general(a_ref[...], ws.tile(0), (((1,), (1,)), ((), ())),
                                  preferred_element_type=F32)
            o_ref[...] = fn(acc).astype(o_ref.dtype)


def _in_proj(h, w_t, n_cols, col_epilogues):
    m, k = h.shape
    bm = _pick(m, (1024, 512, 256, 128, 64))
    bn = _pick(math.gcd(*[end for _, end, _ in col_epilogues]), (1024, 512, 256, 128))
    epilogues = tuple((first // bn, end // bn, fn) for first, end, fn in col_epilogues)
    n_col_steps = n_cols // bn
    return pl.pallas_call(
        functools.partial(_in_proj_kernel, epilogues=epilogues, n_col_steps=n_col_steps),
        grid=(n_col_steps, m // bm),
        in_specs=[
            pl.BlockSpec((bm, k), lambda j, i: (i, 0)),
            pl.BlockSpec(memory_space=pl.ANY),
        ],
        out_specs=pl.BlockSpec((bm, bn), lambda j, i: (i, j)),
        out_shape=jax.ShapeDtypeStruct((m, n_cols), BF16),
        scratch_shapes=_weight_stream_scratch(1, (bn, k), m // bm),
        compiler_params=_params("arbitrary", "arbitrary"),
        name="gemm_in_proj",
    )(h, w_t)


def _sgu_tile(u_ref, v_ref, w_ref, bias_ref, lng_ref, lnb_ref, o_ref, vn_ref, *, chunk):
    rows, width = u_ref.shape
    heads = w_ref.shape[0]
    head_dim = width // heads
    n_chunks = rows // chunk
    v = v_ref[...].astype(F32)
    mu = jnp.mean(v, axis=-1, keepdims=True)
    vc = v - mu
    var = jnp.mean(vc * vc, axis=-1, keepdims=True)
    vn = vc * lax.rsqrt(var + EPS) * lng_ref[...] + lnb_ref[...]
    if vn_ref is not None:
        vn_ref[...] = vn
    vnb = vn.astype(BF16)
    ri = lax.broadcasted_iota(jnp.int32, (chunk, chunk), 0)
    ci = lax.broadcasted_iota(jnp.int32, (chunk, chunk), 1)
    causal = ci <= ri
    for h in range(heads):
        lo, hi = h * head_dim, (h + 1) * head_dim
        w = jnp.where(causal, w_ref[h], 0.0).astype(BF16)
        rhs = jnp.concatenate(
            [vnb[c * chunk:(c + 1) * chunk, lo:hi] for c in range(n_chunks)], axis=1)
        mixed = _dot(w, rhs)
        bias = bias_ref[:, lo:hi]
        for c in range(n_chunks):
            rs = slice(c * chunk, (c + 1) * chunk)
            u = u_ref[rs, lo:hi].astype(F32)
            m = mixed[:, c * head_dim:(c + 1) * head_dim] + bias
            o_ref[rs, lo:hi] = (u * m).astype(o_ref.dtype)


def _mixers_kernel(u_ref, vg_ref, q_ref, k_ref, v_ref, r_ref, b_ref, s0_ref, gn_ref,
                   w_ref, bias_ref, lng_ref, lnb_ref, a_ref, o_ref, s_out_ref, *rest,
                   heads, dk, dv, blocks, chunk, emit_vn):
    vn_ref, s_ref = rest if emit_vn else (None, rest[0])
    t = pl.program_id(1)

    @pl.when(t == 0)
    def _():
        s_ref[...] = s0_ref[0]

    _sgu_tile(u_ref, vg_ref, w_ref, bias_ref, lng_ref, lnb_ref, a_ref, vn_ref, chunk=chunk)

    ri = lax.broadcasted_iota(jnp.int32, (GLA_BLOCK, GLA_BLOCK), 0)
    ci = lax.broadcasted_iota(jnp.int32, (GLA_BLOCK, GLA_BLOCK), 1)
    causal = ci <= ri
    half = GLA_BLOCK // 2
    for blk in range(blocks):
        rs = slice(blk * GLA_BLOCK, (blk + 1) * GLA_BLOCK)
        for h in range(heads):
            ks = slice(h * dk, (h + 1) * dk)
            vs = slice(h * dv, (h + 1) * dv)
            q = q_ref[rs, ks].astype(F32)
            k = k_ref[rs, ks].astype(F32)
            v = v_ref[rs, vs]
            b = b_ref[rs, ks]
            b_mid = b[half:half + 1, :]
            b_last = b[GLA_BLOCK - 1:GLA_BLOCK, :]
            q_in = (q * jnp.exp(b - b_mid)).astype(BF16)
            k_in = (k * jnp.exp(b_mid - b)).astype(BF16)
            att = lax.dot_general(q_in, k_in, (((1,), (1,)), ((), ())),
                                  preferred_element_type=F32)
            att = jnp.where(causal, att, 0.0).astype(BF16)
            s = s_ref[h]
            o = _dot(att, v) + _dot((q * jnp.exp(b)).astype(BF16), s.astype(BF16))
            k_out = (k * jnp.exp(b_last - b)).astype(BF16)
            kv = lax.dot_general(k_out, v, (((0,), (0,)), ((), ())),
                                 preferred_element_type=F32)
            decay_col = jnp.transpose(
                jnp.broadcast_to(jnp.exp(b_last), (LANES, dk)))
            decay = jnp.concatenate([decay_col] * (dv // LANES), axis=1)
            s_ref[h] = decay * s + kv
            o_n = _rms(o, gn_ref[...])
            o_ref[rs, vs] = (o_n * r_ref[rs, vs].astype(F32)).astype(o_ref.dtype)

    @pl.when(t == pl.num_programs(1) - 1)
    def _():
        s_out_ref[0] = s_ref[...]


def _mixers(proj, b_all, s0, gn, w_s, bias_full, ln_g, ln_b, row0, n_streams, stream_len, *,
            col_q, col_k, col_v, col_r, sgu_chunk, emit_vn):
    _, heads, dk, dv = s0.shape
    kd, vd = heads * dk, heads * dv
    a_width = ln_g.shape[1]
    blocks = _pick(stream_len // GLA_BLOCK, (4, 2, 1))
    rows = blocks * GLA_BLOCK
    assert rows % sgu_chunk == 0, "an SGU chunk may not straddle two row tiles"
    steps = stream_len // rows
    r0 = row0 // rows
    cq, ck, cv, cr = col_q // kd, col_k // kd, col_v // vd, col_r // vd
    kern = functools.partial(_mixers_kernel, heads=heads, dk=dk, dv=dv, blocks=blocks,
                             chunk=sgu_chunk, emit_vn=emit_vn)
    row = lambda s, t: r0 + s * steps + t
    out_row = lambda s, t: s * steps + t
    n_rows = n_streams * stream_len
    const = lambda shape: pl.BlockSpec(shape, lambda s, t: (0,) * len(shape))
    out_specs = [
        pl.BlockSpec((rows, a_width), lambda s, t: (out_row(s, t), 0)),
        pl.BlockSpec((rows, vd), lambda s, t: (out_row(s, t), 0)),
        pl.BlockSpec((1, heads, dk, dv), lambda s, t: (s, 0, 0, 0)),
    ]
    out_shape = [
        jax.ShapeDtypeStruct((n_rows, a_width), BF16),
        jax.ShapeDtypeStruct((n_rows, vd), BF16),
        jax.ShapeDtypeStruct(s0.shape, F32),
    ]
    if emit_vn:
        out_specs.append(pl.BlockSpec((rows, a_width), lambda s, t: (out_row(s, t), 0)))
        out_shape.append(jax.ShapeDtypeStruct((n_rows, a_width), F32))
    return pl.pallas_call(
        kern,
        grid=(n_streams, steps),
        in_specs=[
            pl.BlockSpec((rows, a_width), lambda s, t: (row(s, t), 0)),
            pl.BlockSpec((rows, a_width), lambda s, t: (row(s, t), 1)),
            pl.BlockSpec((rows, kd), lambda s, t: (row(s, t), cq)),
            pl.BlockSpec((rows, kd), lambda s, t: (row(s, t), ck)),
            pl.BlockSpec((rows, vd), lambda s, t: (row(s, t), cv)),
            pl.BlockSpec((rows, vd), lambda s, t: (row(s, t), cr)),
            pl.BlockSpec((rows, kd), lambda s, t: (row(s, t), 0)),
            pl.BlockSpec((1, heads, dk, dv), lambda s, t: (s, 0, 0, 0)),
            const((1, dv)), const(w_s.shape), const(bias_full.shape),
            const((1, a_width)), const((1, a_width)),
        ],
        out_specs=out_specs,
        out_shape=out_shape,
        scratch_shapes=[pltpu.VMEM((heads, dk, dv), F32)],
        compiler_params=_params("arbitrary", "arbitrary"),
        name="sgu_gla_mixers",
    )(proj, proj, proj, proj, proj, proj, b_all, s0, gn, w_s, bias_full, ln_g, ln_b)


def _mix_kernel(ap_ref, as_ref, bp_ref, bs_ref, xp_ref, xs_ref, wa_ref, wb_ref, g_ref,
                x1_ref, xg_ref, ssq_ref, *, p_tiles):
    j = pl.program_id(1)

    def body(a_ref, b_ref, x_ref):
        x1 = x_ref[...] + (_dot(a_ref[...], wa_ref[...]) + _dot(b_ref[...], wb_ref[...]))
        x1_ref[...] = x1
        xg_ref[...] = (x1 * g_ref[...]).astype(xg_ref.dtype)
        sq = x1 * x1
        part = sq[:, :LANES]
        for c in range(1, sq.shape[1] // LANES):
            part = part + sq[:, c * LANES:(c + 1) * LANES]

        ssq_ref[...] += part

    @pl.when(j == 0)
    def _():
        ssq_ref[...] = jnp.zeros_like(ssq_ref)

    _on_prompt_or_sample(pl.program_id(0), p_tiles, body,
                         (ap_ref, bp_ref, xp_ref), (as_ref, bs_ref, xs_ref))


def _mix(a_p, a_s, b_p, b_s, x_p, x_s, w_out, g):
    (n_p, ka), n_s = a_p.shape, a_s.shape[0]
    m = n_p + n_s
    kb = b_p.shape[1]
    n = w_out.shape[1]
    assert ka == kb, "head groups of different widths need separate weight specs"
    bm = _pick(math.gcd(n_p, n_s), (1024, 512, 256, 128, 64))
    bn = _pick(n, (512, 256, 128))
    p_tiles = n_p // bm
    split = functools.partial(_split_specs, p_tiles=p_tiles, single_buffer_sample=True)
    return pl.pallas_call(
        functools.partial(_mix_kernel, p_tiles=p_tiles),
        grid=(m // bm, n // bn),
        in_specs=(split((bm, ka)) + split((bm, kb)) + split((bm, bn), col_tiles=n // bn) + [
            pl.BlockSpec((ka, bn), lambda i, j: (0, j)),
            pl.BlockSpec((kb, bn), lambda i, j: (1, j)),
            pl.BlockSpec((1, bn), lambda i, j: (0, j)),
        ]),
        out_specs=[
            pl.BlockSpec((bm, bn), lambda i, j: (i, j)),
            pl.BlockSpec((bm, bn), lambda i, j: (i, j)),
            pl.BlockSpec((bm, LANES), lambda i, j: (i, 0)),
        ],
        out_shape=[
            jax.ShapeDtypeStruct((m, n), F32),
            jax.ShapeDtypeStruct((m, n), BF16),
            jax.ShapeDtypeStruct((m, LANES), F32),
        ],
        compiler_params=_params("parallel", "arbitrary"),
        name="gemm_out_proj_residual",
    )(a_p, a_s, b_p, b_s, x_p, x_s, w_out, w_out, g)


def _rmsnorm_kernel(x_ref, g_ref, o_ref):
    o_ref[...] = _rms(x_ref[...], g_ref[...]).astype(o_ref.dtype)


def _rmsnorm(x, g, out_dtype, row0=0, n_rows=None):
    n, d = x.shape
    n_rows = n if n_rows is None else n_rows
    bm = _pick(math.gcd(n_rows, row0) if row0 else n_rows, (256, 128, 64))
    r0 = row0 // bm
    return pl.pallas_call(
        _rmsnorm_kernel,
        grid=(n_rows // bm,),
        in_specs=[
            pl.BlockSpec((bm, d), lambda i: (r0 + i, 0)),
            pl.BlockSpec((1, d), lambda i: (0, 0)),
        ],
        out_specs=pl.BlockSpec((bm, d), lambda i: (i, 0)),
        out_shape=jax.ShapeDtypeStruct((n_rows, d), out_dtype),
        compiler_params=_params("parallel"),
        name="rmsnorm",
    )(x, g)


def _ffn_up_kernel(xg_ref, ssq_ref, wg_hbm, wu_hbm, o_ref, wbuf, stage, sem, *, d_model, d_ff):
    bn = o_ref.shape[1]
    n_steps = -(-d_ff // bn)
    edge = d_ff - (n_steps - 1) * bn
    edge_first = edge != bn
    ws = _WeightStream([wg_hbm, wu_hbm], wbuf, stage, sem, bn=bn, n_col_steps=n_steps,
                       first_col=n_steps - 1 if edge_first else 0, first_width=edge,
                       next_col=(lambda j: j) if edge_first else (lambda j: j + 1))
    ws.advance()
    rinv = lax.rsqrt(jnp.sum(ssq_ref[...], axis=-1, keepdims=True) * (1.0 / d_model) + EPS)

    def body(width):
        ws.cast_arrived_chunk()
        h = xg_ref[...]
        g = _dot(h, ws.tile(0, width)) * rinv
        u = _dot(h, ws.tile(1, width)) * rinv
        o_ref[:, :width] = (_silu(g) * u).astype(o_ref.dtype)

    if edge_first:
        pl.when(ws.j == 0)(lambda: body(edge))
        pl.when(ws.j > 0)(lambda: body(bn))
    else:
        body(bn)


def _ffn_up(xg, ssq, wg, wu):
    m, k = xg.shape
    f = wg.shape[1]
    bm = _pick(m, (1024, 512, 256, 128, 64))
    bn = 512
    assert (f % bn) % 256 == 0, "the edge column block must stay MXU-column aligned"
    n_steps = pl.cdiv(f, bn)
    col = (lambda j: j) if f % bn == 0 else (lambda j: jnp.where(j == 0, n_steps - 1, j - 1))
    return pl.pallas_call(
        functools.partial(_ffn_up_kernel, d_model=k, d_ff=f),
        grid=(n_steps, m // bm),
        in_specs=[
            pl.BlockSpec((bm, k), lambda j, i: (i, 0)),
            pl.BlockSpec((bm, LANES), lambda j, i: (i, 0)),
            pl.BlockSpec(memory_space=pl.ANY),
            pl.BlockSpec(memory_space=pl.ANY),
        ],
        out_specs=pl.BlockSpec((bm, bn), lambda j, i: (i, col(j))),
        out_shape=jax.ShapeDtypeStruct((m, f), BF16),
        scratch_shapes=_weight_stream_scratch(2, (k, bn), m // bm),
        compiler_params=_params("arbitrary", "arbitrary"),
        name="ffn_gate_up",
    )(xg, ssq, wg, wu)


def _ffn_down_kernel(h_ref, w_ref, x_ref, o_ref, *, d_ff):
    k = pl.program_id(2)
    bk = h_ref.shape[1]
    n_steps = -(-d_ff // bk)
    last = d_ff - (n_steps - 1) * bk

    def body(width, first):
        p = _dot(h_ref[:, :width], w_ref[:width, :])
        if first:
            o_ref[...] = x_ref[...] + p
        else:
            o_ref[...] += p

    if n_steps == 1:
        body(last, True)
        return
    pl.when(k == 0)(lambda: body(bk, True))
    if n_steps > 2:
        pl.when((k > 0) & (k < n_steps - 1))(lambda: body(bk, False))
    pl.when(k == n_steps - 1)(lambda: body(last, False))


def _ffn_down(h1, wd, x1):
    m, f = h1.shape
    n = wd.shape[1]
    bm = _pick(m, (1024, 512, 256, 128, 64))
    bn = _pick(n, (1024, 512, 256, 128))
    bk = 2816
    assert (f % bk) % 256 == 0, "the last contraction block must stay MXU-row aligned"
    return pl.pallas_call(
        functools.partial(_ffn_down_kernel, d_ff=f),
        grid=(m // bm, n // bn, pl.cdiv(f, bk)),
        in_specs=[
            pl.BlockSpec((bm, bk), lambda i, j, k: (i, k)),
            pl.BlockSpec((bk, bn), lambda i, j, k: (k, j)),
            pl.BlockSpec((bm, bn), lambda i, j, k: (i, j)),
        ],
        out_specs=pl.BlockSpec((bm, bn), lambda i, j, k: (i, j)),
        out_shape=jax.ShapeDtypeStruct((m, n), F32),
        compiler_params=_params("parallel", "parallel", "arbitrary"),
        name="ffn_down_residual",
    )(h1, wd, x1)


def kernel(x_prompt, x_sample, state_gla, g_mix, w_in, w_s, b_s, ln_g, ln_b, w_gate_up, b_gate,
           gla_norm_g, w_out, g_ffn, w_ffn_gate, w_ffn_up, w_ffn_down, g_final):
    depth = g_mix.shape[0]
    n_p, len_p, d_model = x_prompt.shape
    n_s, len_s, _ = x_sample.shape
    rows_p, rows_s = n_p * len_p, n_s * len_s
    _, _, heads_b, dk, dv = state_gla.shape
    a_width = ln_g.shape[-1]
    a_heads, sgu_chunk = w_s.shape[1], w_s.shape[2]
    key_dim, val_dim = heads_b * dk, heads_b * dv
    rank = w_gate_up.shape[1]
    d_ff = w_ffn_gate.shape[-1]
    main_cols = 2 * a_width + 2 * key_dim + 2 * val_dim
    col_q = 2 * a_width
    col_k = col_q + key_dim
    col_v = col_k + key_dim
    col_r = col_v + val_dim
    chunk_p = min(sgu_chunk, len_p)
    chunk_s = min(sgu_chunk, len_s)

    x_p = x_prompt.reshape(rows_p, d_model)
    x_s = x_sample.reshape(rows_s, d_model)
    zero_state = jnp.zeros((n_p, heads_b, dk, dv), state_gla.dtype)

    gla_p, gla_s, vn_s = [], [], []
    for d in range(depth):
        w_up = jnp.pad(w_gate_up[d], ((0, LANES - rank), (0, 0))).astype(BF16)
        w_o = w_out[d].astype(BF16)
        wd = w_ffn_down[d].astype(BF16)
        head_dim = a_width // a_heads
        bias_full = jnp.repeat(b_s[d].T, head_dim, axis=1)
        row = lambda v: v.reshape(1, -1)

        w_in_t = w_in[d].T
        h, b_all = _prep(x_p, x_s, row(g_mix[d]), w_in_t, main_cols, rank, w_up, row(b_gate[d]))
        q_scale = dk ** -0.5
        proj = _in_proj(h, w_in_t, main_cols, (
            (0, col_q, _gelu_tanh),
            (col_q, col_k, lambda t: t * q_scale),
            (col_k, col_r, lambda t: t),
            (col_r, main_cols, _silu),
        ))

        cols = dict(col_q=col_q, col_k=col_k, col_v=col_v, col_r=col_r)
        shared = (row(gla_norm_g[d]),)
        ln = (row(ln_g[d]), row(ln_b[d]))
        a_p, o_p, s_p = _mixers(
            proj, b_all, zero_state, *shared, w_s[d][:, :chunk_p, :chunk_p], bias_full[:chunk_p],
            *ln, 0, n_p, len_p, **cols, sgu_chunk=chunk_p, emit_vn=False)
        a_s, o_s, s_s, vn = _mixers(
            proj, b_all, state_gla[d], *shared, w_s[d][:, :chunk_s, :chunk_s], bias_full[:chunk_s],
            *ln, rows_p, n_s, len_s, **cols, sgu_chunk=chunk_s, emit_vn=True)

        x1, xg, ssq = _mix(a_p, a_s, o_p, o_s, x_p, x_s, w_o, row(g_ffn[d]))
        h1 = _ffn_up(xg, ssq, w_ffn_gate[d], w_ffn_up[d])
        x = _ffn_down(h1, wd, x1)
        if d + 1 < depth:
            x_p, x_s = x[:rows_p], x[rows_p:]

        gla_p.append(s_p)
        gla_s.append(s_s)
        vn_s.append(vn.reshape(n_s, len_s, a_width))

    gf = g_final.reshape(1, -1)
    y_p = _rmsnorm(x, gf, F32, 0, rows_p).reshape(n_p, len_p, d_model)
    y_s = _rmsnorm(x, gf, F32, rows_p, rows_s).reshape(n_s, len_s, d_model)
    return (y_p, y_s, jnp.stack(gla_p), jnp.stack(gla_s), jnp.stack(vn_s))
```

```python
import functools
import math

import jax
import jax.numpy as jnp
from jax import lax
from jax.experimental import pallas as pl
from jax.experimental.pallas import tpu as pltpu

F32 = jnp.float32
BF16 = jnp.bfloat16

EPS = 1e-6
GLA_BLOCK = 64
GATE_TAU = 16.0
LANES = 128
V7X_VMEM_LIMIT_BYTES = 56 * 1024 * 1024


def _pick(n, candidates):
    for c in candidates:
        if n % c == 0:
            return c
    raise ValueError(f"no tile in {candidates} divides {n}")


def _params(*semantics):
    return pltpu.CompilerParams(dimension_semantics=semantics,
                                vmem_limit_bytes=V7X_VMEM_LIMIT_BYTES)


def _dot(a, b):
    return jnp.dot(a, b, preferred_element_type=F32)


def _gelu_tanh(x):
    c = math.sqrt(2.0 / math.pi)
    return 0.5 * x * (1.0 + jnp.tanh(c * (x + 0.044715 * (x * x * x))))


def _silu(x):
    return x * (1.0 / (1.0 + jnp.exp(-x)))


def _rms(x, g):
    return x * lax.rsqrt(jnp.mean(x * x, axis=-1, keepdims=True) + EPS) * g


def _prep_kernel(xp_ref, xs_ref, g_ref, win_ref, wup_ref, bg_ref, tri_ref, h_ref, b_ref, wlr_ref,
                 *, p_tiles, sub, rank):
    @pl.when(pl.program_id(0) == 0)
    def _():
        pad = jnp.zeros((LANES - rank, win_ref.shape[1]), F32)
        wlr_ref[...] = jnp.transpose(jnp.concatenate([win_ref[...], pad], axis=0)).astype(BF16)

    def body(x_ref):
        for c in range(x_ref.shape[0] // sub):
            rs = slice(c * sub, (c + 1) * sub)
            hb = _rms(x_ref[rs, :], g_ref[...]).astype(BF16)
            h_ref[rs, :] = hb
            g_lr = _dot(hb, wlr_ref[...])
            z = _dot(g_lr.astype(BF16), wup_ref[...]) + bg_ref[...]
            log_a = (jnp.minimum(z, 0.0) - jnp.log1p(jnp.exp(-jnp.abs(z)))) * (1.0 / GATE_TAU)
            hi = log_a.astype(BF16)
            r1 = log_a - hi.astype(F32)
            mid = r1.astype(BF16)
            lo = (r1 - mid.astype(F32)).astype(BF16)
            tri = tri_ref[...]
            b_ref[rs, :] = _dot(tri, hi) + _dot(tri, mid) + _dot(tri, lo)

    _on_prompt_or_sample(pl.program_id(0), p_tiles, body, (xp_ref,), (xs_ref,))


def _on_prompt_or_sample(i, p_tiles, body, prompt_refs, sample_refs):
    @pl.when(i < p_tiles)
    def _():
        body(*prompt_refs)

    @pl.when(i >= p_tiles)
    def _():
        body(*sample_refs)


def _split_specs(block, p_tiles, col_tiles=None, single_buffer_sample=False):
    def prompt_map(i, *g):
        col = jnp.where(i < p_tiles, g[0], col_tiles - 1) if col_tiles else 0
        return jnp.minimum(i, p_tiles - 1), col

    def sample_map(i, *g):
        col = jnp.where(i >= p_tiles, g[0], 0) if col_tiles else 0
        return jnp.maximum(i - p_tiles, 0), col

    mode = dict(pipeline_mode=pl.Buffered(1)) if single_buffer_sample else {}
    return [pl.BlockSpec(block, prompt_map), pl.BlockSpec(block, sample_map, **mode)]


def _prep(x_p, x_s, g, w_in_t, gate_row0, rank, w_up, b_gate):
    (n_p, d), n_s = x_p.shape, x_s.shape[0]
    n = n_p + n_s
    key_dim = w_up.shape[1]
    assert gate_row0 % rank == 0 and rank % 8 == 0 and rank <= LANES
    bm = _pick(math.gcd(n_p, n_s), (512, 256, 128, 64))
    sub = _pick(bm, (128, 64))
    idx = jnp.arange(sub)
    tri = ((idx[None, :] <= idx[:, None])
           & (idx[None, :] // GLA_BLOCK == idx[:, None] // GLA_BLOCK)).astype(BF16)
    p_tiles = n_p // bm
    const = lambda shape: pl.BlockSpec(shape, lambda i: (0, 0))
    return pl.pallas_call(
        functools.partial(_prep_kernel, p_tiles=p_tiles, sub=sub, rank=rank),
        grid=(n // bm,),
        in_specs=_split_specs((bm, d), p_tiles, single_buffer_sample=True) + [
            const((1, d)),
            pl.BlockSpec((rank, d), lambda i: (gate_row0 // rank, 0)),
            const(w_up.shape), const((1, key_dim)), const((sub, sub)),
        ],
        out_specs=[
            pl.BlockSpec((bm, d), lambda i: (i, 0)),
            pl.BlockSpec((bm, key_dim), lambda i: (i, 0)),
        ],
        out_shape=[
            jax.ShapeDtypeStruct((n, d), BF16),
            jax.ShapeDtypeStruct((n, key_dim), F32),
        ],
        scratch_shapes=[pltpu.VMEM((d, LANES), BF16)],
        compiler_params=_params("arbitrary"),
        name="prep_rmsnorm_gate",
    )(x_p, x_s, g, w_in_t, w_up, b_gate, tri)


class _WeightStream:
    def __init__(self, w_hbms, wbuf, stage, sem, *, bn, n_col_steps, first_col, first_width,
                 next_col, transposed=False):
        self.w_hbms, self.wbuf, self.stage, self.sem = w_hbms, wbuf, stage, sem
        self.transposed = transposed
        self.bn = bn
        self.rows = stage.shape[2]
        if transposed:
            self.n_chunks = wbuf.shape[2] - 1
        else:
            self.tile_rows = wbuf.shape[2] - self.rows
            self.n_chunks = self.tile_rows // self.rows
        self.n_col_steps = n_col_steps
        self.first_col, self.first_width = first_col, first_width
        self.next_col = next_col
        self.j = pl.program_id(0)
        self.i = pl.program_id(1)

    def _copy(self, m, chunk, col0, width, slot):
        if self.transposed:
            assert width == self.bn
            src = self.w_hbms[m].at[pl.ds(col0 + chunk * self.rows, self.rows), :]
            dst = self.stage.at[m, slot]
        else:
            src = self.w_hbms[m].at[pl.ds(chunk * self.rows, self.rows), pl.ds(col0, width)]
            dst = self.stage.at[m, slot, :, pl.ds(0, width)]
        return pltpu.make_async_copy(src, dst, self.sem.at[m, slot])

    def _next_copy(self, m, chunk):
        col0 = pl.multiple_of(self.next_col(self.j) * self.bn, self.bn)
        return self._copy(m, chunk, col0, self.bn, chunk % 2)

    def advance(self):
        j, i, n_mats = self.j, self.i, len(self.w_hbms)

        @pl.when((j == 0) & (i == 0))
        def _():
            self.stage[...] = jnp.zeros_like(self.stage)
            for c in range(self.n_chunks):
                for m in range(n_mats):
                    cp = self._copy(m, c, self.first_col * self.bn, self.first_width, c % 2)
                    cp.start()
                    cp.wait()
                    self._store_chunk(m, 0, c, c % 2)

        has_next = j + 1 < self.n_col_steps

        @pl.when(has_next & (i >= 1) & (i <= self.n_chunks))
        def _():
            for m in range(n_mats):
                self._next_copy(m, i - 1).wait()

        @pl.when(has_next & (i < self.n_chunks))
        def _():
            for m in range(n_mats):
                self._next_copy(m, i).start()

    def _store_chunk(self, m, tile_slot, chunk, stage_slot):
        landed = self.stage[m, stage_slot]
        if self.transposed:
            self.wbuf[m, tile_slot, chunk] = jnp.transpose(landed).astype(BF16)
        else:
            row0 = chunk * self.rows
            if not isinstance(row0, int):
                row0 = pl.multiple_of(row0, self.rows)
            self.wbuf[m, tile_slot, pl.ds(row0, self.rows), :] = landed.astype(BF16)

    def cast_arrived_chunk(self):
        j, i = self.j, self.i
        valid = (j + 1 < self.n_col_steps) & (i >= 1) & (i <= self.n_chunks)
        chunk = jnp.where(valid, i - 1, self.n_chunks)
        for m in range(len(self.w_hbms)):
            self._store_chunk(m, (j + 1) % 2, chunk, (i + 1) % 2)

    def tile(self, m, width=None):
        slot = self.j % 2
        if self.transposed:
            w = jnp.concatenate([self.wbuf[m, slot, c] for c in range(self.n_chunks)], axis=1)
        else:
            w = self.wbuf[m, slot, :self.tile_rows, :]
        return w if width is None else w[:, :width]


def _weight_stream_scratch(n_mats, k, bn, n_row_steps, transposed=False):
    assert n_row_steps >= 2, "weight chunks are prefetched across the row steps of one column tile"
    chunked, align = (bn, LANES) if transposed else (k, 16)
    n_chunks = 1
    while n_chunks * 2 <= min(n_row_steps - 1, 16) and chunked % (n_chunks * 2 * align) == 0:
        n_chunks *= 2
    rows = chunked // n_chunks
    if transposed:
        tile, landing = (n_chunks + 1, k, rows), (rows, k)
    else:
        tile, landing = (k + rows, bn), (rows, bn)
    return [
        pltpu.VMEM((n_mats, 2) + tile, BF16),
        pltpu.VMEM((n_mats, 2) + landing, F32),
        pltpu.SemaphoreType.DMA((n_mats, 2)),
    ]


def _in_proj_kernel(a_ref, wt_hbm, o_ref, wbuf, stage, sem, *, epilogues, n_col_steps):
    bn = o_ref.shape[1]
    ws = _WeightStream([wt_hbm], wbuf, stage, sem, bn=bn, n_col_steps=n_col_steps,
                       first_col=0, first_width=bn, next_col=lambda j: j + 1, transposed=True)
    ws.advance()
    for first, end, fn in epilogues:
        @pl.when((ws.j >= first) & (ws.j < end))
        def _(fn=fn):
            ws.cast_arrived_chunk()
            o_ref[...] = fn(_dot(a_ref[...], ws.tile(0))).astype(o_ref.dtype)


def _in_proj(h, w_t, n_cols, col_epilogues):
    m, k = h.shape
    bm = _pick(m, (1024, 512, 256, 128, 64))
    bn = _pick(math.gcd(*[end for _, end, _ in col_epilogues]), (1024, 512, 256, 128))
    epilogues = tuple((first // bn, end // bn, fn) for first, end, fn in col_epilogues)
    n_col_steps = n_cols // bn
    return pl.pallas_call(
        functools.partial(_in_proj_kernel, epilogues=epilogues, n_col_steps=n_col_steps),
        grid=(n_col_steps, m // bm),
        in_specs=[
            pl.BlockSpec((bm, k), lambda j, i: (i, 0)),
            pl.BlockSpec(memory_space=pl.ANY),
        ],
        out_specs=pl.BlockSpec((bm, bn), lambda j, i: (i, j)),
        out_shape=jax.ShapeDtypeStruct((m, n_cols), BF16),
        scratch_shapes=_weight_stream_scratch(1, k, bn, m // bm, transposed=True),
        compiler_params=_params("arbitrary", "arbitrary"),
        name="gemm_in_proj",
    )(h, w_t)


def _sgu_tile(u_ref, v_ref, wm_ref, bias_ref, lng_ref, lnb_ref, o_ref, vn_ref):
    rows, width = u_ref.shape
    heads, chunk, _ = wm_ref.shape
    head_dim = width // heads
    n_chunks = rows // chunk
    v = v_ref[...].astype(F32)
    mu = jnp.mean(v, axis=-1, keepdims=True)
    vc = v - mu
    var = jnp.mean(vc * vc, axis=-1, keepdims=True)
    vn = vc * lax.rsqrt(var + EPS) * lng_ref[...] + lnb_ref[...]
    if vn_ref is not None:
        vn_ref[...] = vn
    vnb = vn.astype(BF16)
    for h in range(heads):
        lo, hi = h * head_dim, (h + 1) * head_dim
        rhs = jnp.concatenate(
            [vnb[c * chunk:(c + 1) * chunk, lo:hi] for c in range(n_chunks)], axis=1)
        mixed = _dot(wm_ref[h], rhs)
        bias = bias_ref[:, lo:hi]
        for c in range(n_chunks):
            rs = slice(c * chunk, (c + 1) * chunk)
            u = u_ref[rs, lo:hi].astype(F32)
            m = mixed[:, c * head_dim:(c + 1) * head_dim] + bias
            o_ref[rs, lo:hi] = (u * m).astype(o_ref.dtype)


def _mixers_kernel(u_ref, vg_ref, q_ref, k_ref, v_ref, r_ref, b_ref, s0_ref, gn_ref,
                   w_ref, bias_ref, lng_ref, lnb_ref, a_ref, o_ref, s_out_ref, *rest,
                   heads, dk, dv, blocks, emit_vn):
    vn_ref, s_ref, wm_ref = rest if emit_vn else (None,) + rest
    t = pl.program_id(1)

    @pl.when((pl.program_id(0) == 0) & (t == 0))
    def _():
        chunk = w_ref.shape[1]
        ri = lax.broadcasted_iota(jnp.int32, (chunk, chunk), 0)
        ci = lax.broadcasted_iota(jnp.int32, (chunk, chunk), 1)
        for h in range(w_ref.shape[0]):
            wm_ref[h] = jnp.where(ci <= ri, w_ref[h], 0.0).astype(BF16)

    @pl.when(t == 0)
    def _():
        s_ref[...] = s0_ref[0]

    _sgu_tile(u_ref, vg_ref, wm_ref, bias_ref, lng_ref, lnb_ref, a_ref, vn_ref)

    ri = lax.broadcasted_iota(jnp.int32, (GLA_BLOCK, GLA_BLOCK), 0)
    ci = lax.broadcasted_iota(jnp.int32, (GLA_BLOCK, GLA_BLOCK), 1)
    causal = ci <= ri
    half = GLA_BLOCK // 2
    for blk in range(blocks):
        rs = slice(blk * GLA_BLOCK, (blk + 1) * GLA_BLOCK)
        for h in range(heads):
            ks = slice(h * dk, (h + 1) * dk)
            vs = slice(h * dv, (h + 1) * dv)
            q = q_ref[rs, ks].astype(F32)
            k = k_ref[rs, ks].astype(F32)
            v = v_ref[rs, vs]
            b = b_ref[rs, ks]
            b_mid = b[half:half + 1, :]
            b_last = b[GLA_BLOCK - 1:GLA_BLOCK, :]
            q_in = (q * jnp.exp(b - b_mid)).astype(BF16)
            k_in = (k * jnp.exp(b_mid - b)).astype(BF16)
            att = lax.dot_general(q_in, k_in, (((1,), (1,)), ((), ())),
                                  preferred_element_type=F32)
            att = jnp.where(causal, att, 0.0).astype(BF16)
            s = s_ref[h]
            o = _dot(att, v) + _dot((q * jnp.exp(b)).astype(BF16), s.astype(BF16))
            k_out = (k * jnp.exp(b_last - b)).astype(BF16)
            kv = lax.dot_general(k_out, v, (((0,), (0,)), ((), ())),
                                 preferred_element_type=F32)
            decay_col = jnp.transpose(
                jnp.broadcast_to(jnp.exp(b_last), (LANES, dk)))
            decay = jnp.concatenate([decay_col] * (dv // LANES), axis=1)
            s_ref[h] = decay * s + kv
            o_n = _rms(o, gn_ref[...])
            o_ref[rs, vs] = (o_n * r_ref[rs, vs].astype(F32)).astype(o_ref.dtype)

    @pl.when(t == pl.num_programs(1) - 1)
    def _():
        s_out_ref[0] = s_ref[...]


def _mixers(proj, b_all, s0, gn, w_s, bias_full, ln_g, ln_b, row0, n_streams, stream_len, *,
            col_q, col_k, col_v, col_r, sgu_chunk, emit_vn):
    _, heads, dk, dv = s0.shape
    kd, vd = heads * dk, heads * dv
    a_width = ln_g.shape[1]
    blocks = _pick(stream_len // GLA_BLOCK, (4, 2, 1))
    rows = blocks * GLA_BLOCK
    assert rows % sgu_chunk == 0, "an SGU chunk may not straddle two row tiles"
    steps = stream_len // rows
    r0 = row0 // rows
    cq, ck, cv, cr = col_q // kd, col_k // kd, col_v // vd, col_r // vd
    assert w_s.shape[1:] == (sgu_chunk, sgu_chunk)
    kern = functools.partial(_mixers_kernel, heads=heads, dk=dk, dv=dv, blocks=blocks,
                             emit_vn=emit_vn)
    row = lambda s, t: r0 + s * steps + t
    out_row = lambda s, t: s * steps + t
    n_rows = n_streams * stream_len
    const = lambda shape: pl.BlockSpec(shape, lambda s, t: (0,) * len(shape))
    out_specs = [
        pl.BlockSpec((rows, a_width), lambda s, t: (out_row(s, t), 0)),
        pl.BlockSpec((rows, vd), lambda s, t: (out_row(s, t), 0)),
        pl.BlockSpec((1, heads, dk, dv), lambda s, t: (s, 0, 0, 0)),
    ]
    out_shape = [
        jax.ShapeDtypeStruct((n_rows, a_width), BF16),
        jax.ShapeDtypeStruct((n_rows, vd), BF16),
        jax.ShapeDtypeStruct(s0.shape, F32),
    ]
    if emit_vn:
        out_specs.append(pl.BlockSpec((rows, a_width), lambda s, t: (out_row(s, t), 0)))
        out_shape.append(jax.ShapeDtypeStruct((n_rows, a_width), F32))
    return pl.pallas_call(
        kern,
        grid=(n_streams, steps),
        in_specs=[
            pl.BlockSpec((rows, a_width), lambda s, t: (row(s, t), 0)),
            pl.BlockSpec((rows, a_width), lambda s, t: (row(s, t), 1)),
            pl.BlockSpec((rows, kd), lambda s, t: (row(s, t), cq)),
            pl.BlockSpec((rows, kd), lambda s, t: (row(s, t), ck)),
            pl.BlockSpec((rows, vd), lambda s, t: (row(s, t), cv)),
            pl.BlockSpec((rows, vd), lambda s, t: (row(s, t), cr)),
            pl.BlockSpec((rows, kd), lambda s, t: (row(s, t), 0)),
            pl.BlockSpec((1, heads, dk, dv), lambda s, t: (s, 0, 0, 0)),
            const((1, dv)), const(w_s.shape), const(bias_full.shape),
            const((1, a_width)), const((1, a_width)),
        ],
        out_specs=out_specs,
        out_shape=out_shape,
        scratch_shapes=[pltpu.VMEM((heads, dk, dv), F32), pltpu.VMEM(w_s.shape, BF16)],
        compiler_params=_params("arbitrary", "arbitrary"),
        name="sgu_gla_mixers",
    )(proj, proj, proj, proj, proj, proj, b_all, s0, gn, w_s, bias_full, ln_g, ln_b)


def _mix_kernel(ap_ref, as_ref, bp_ref, bs_ref, xp_ref, xs_ref, wa_ref, wb_ref, g_ref,
                x1_ref, xg_ref, ssq_ref, *, p_tiles):
    j = pl.program_id(1)

    def body(a_ref, b_ref, x_ref):
        x1 = x_ref[...] + (_dot(a_ref[...], wa_ref[...]) + _dot(b_ref[...], wb_ref[...]))
        x1_ref[...] = x1
        xg_ref[...] = (x1 * g_ref[...]).astype(xg_ref.dtype)
        sq = x1 * x1
        part = sq[:, :LANES]
        for c in range(1, sq.shape[1] // LANES):
            part = part + sq[:, c * LANES:(c + 1) * LANES]

        ssq_ref[...] += part

    @pl.when(j == 0)
    def _():
        ssq_ref[...] = jnp.zeros_like(ssq_ref)

    _on_prompt_or_sample(pl.program_id(0), p_tiles, body,
                         (ap_ref, bp_ref, xp_ref), (as_ref, bs_ref, xs_ref))


def _mix(a_p, a_s, b_p, b_s, x_p, x_s, w_out, g):
    (n_p, ka), n_s = a_p.shape, a_s.shape[0]
    m = n_p + n_s
    kb = b_p.shape[1]
    n = w_out.shape[1]
    assert ka == kb, "head groups of different widths need separate weight specs"
    bm = _pick(math.gcd(n_p, n_s), (1024, 512, 256, 128, 64))
    bn = _pick(n, (512, 256, 128))
    p_tiles = n_p // bm
    split = functools.partial(_split_specs, p_tiles=p_tiles, single_buffer_sample=True)
    return pl.pallas_call(
        functools.partial(_mix_kernel, p_tiles=p_tiles),
        grid=(m // bm, n // bn),
        in_specs=(split((bm, ka)) + split((bm, kb)) + split((bm, bn), col_tiles=n // bn) + [
            pl.BlockSpec((ka, bn), lambda i, j: (0, j)),
            pl.BlockSpec((kb, bn), lambda i, j: (1, j)),
            pl.BlockSpec((1, bn), lambda i, j: (0, j)),
        ]),
        out_specs=[
            pl.BlockSpec((bm, bn), lambda i, j: (i, j)),
            pl.BlockSpec((bm, bn), lambda i, j: (i, j)),
            pl.BlockSpec((bm, LANES), lambda i, j: (i, 0)),
        ],
        out_shape=[
            jax.ShapeDtypeStruct((m, n), F32),
            jax.ShapeDtypeStruct((m, n), BF16),
            jax.ShapeDtypeStruct((m, LANES), F32),
        ],
        compiler_params=_params("parallel", "arbitrary"),
        name="gemm_out_proj_residual",
    )(a_p, a_s, b_p, b_s, x_p, x_s, w_out, w_out, g)


def _rmsnorm_kernel(x_ref, g_ref, o_ref):
    o_ref[...] = _rms(x_ref[...], g_ref[...]).astype(o_ref.dtype)


def _rmsnorm(x, g, out_dtype, row0=0, n_rows=None):
    n, d = x.shape
    n_rows = n if n_rows is None else n_rows
    bm = _pick(math.gcd(n_rows, row0) if row0 else n_rows, (256, 128, 64))
    r0 = row0 // bm
    return pl.pallas_call(
        _rmsnorm_kernel,
        grid=(n_rows // bm,),
        in_specs=[
            pl.BlockSpec((bm, d), lambda i: (r0 + i, 0)),
            pl.BlockSpec((1, d), lambda i: (0, 0)),
        ],
        out_specs=pl.BlockSpec((bm, d), lambda i: (i, 0)),
        out_shape=jax.ShapeDtypeStruct((n_rows, d), out_dtype),
        compiler_params=_params("parallel"),
        name="rmsnorm",
    )(x, g)


def _ffn_up_kernel(xg_ref, ssq_ref, wg_hbm, wu_hbm, o_ref, wbuf, stage, sem, *, d_model, d_ff):
    bn = o_ref.shape[1]
    n_steps = -(-d_ff // bn)
    edge = d_ff - (n_steps - 1) * bn
    edge_first = edge != bn
    ws = _WeightStream([wg_hbm, wu_hbm], wbuf, stage, sem, bn=bn, n_col_steps=n_steps,
                       first_col=n_steps - 1 if edge_first else 0, first_width=edge,
                       next_col=(lambda j: j) if edge_first else (lambda j: j + 1))
    ws.advance()

    def body(width):
        ws.cast_arrived_chunk()
        rinv = lax.rsqrt(jnp.sum(ssq_ref[...], axis=-1, keepdims=True) * (1.0 / d_model) + EPS)
        h = xg_ref[...]
        g = _dot(h, ws.tile(0, width)) * rinv
        u = _dot(h, ws.tile(1, width)) * rinv
        o_ref[:, :width] = (_silu(g) * u).astype(o_ref.dtype)

    if edge_first:
        pl.when(ws.j == 0)(lambda: body(edge))
        pl.when(ws.j > 0)(lambda: body(bn))
    else:
        body(bn)


def _ffn_up(xg, ssq, wg, wu):
    m, k = xg.shape
    f = wg.shape[1]
    bm = _pick(m, (1024, 512, 256, 128, 64))
    bn = 512
    assert (f % bn) % 256 == 0, "the edge column block must stay MXU-column aligned"
    n_steps = pl.cdiv(f, bn)
    col = (lambda j: j) if f % bn == 0 else (lambda j: jnp.where(j == 0, n_steps - 1, j - 1))
    return pl.pallas_call(
        functools.partial(_ffn_up_kernel, d_model=k, d_ff=f),
        grid=(n_steps, m // bm),
        in_specs=[
            pl.BlockSpec((bm, k), lambda j, i: (i, 0)),
            pl.BlockSpec((bm, LANES), lambda j, i: (i, 0)),
            pl.BlockSpec(memory_space=pl.ANY),
            pl.BlockSpec(memory_space=pl.ANY),
        ],
        out_specs=pl.BlockSpec((bm, bn), lambda j, i: (i, col(j))),
        out_shape=jax.ShapeDtypeStruct((m, f), BF16),
        scratch_shapes=_weight_stream_scratch(2, k, bn, m // bm),
        compiler_params=_params("arbitrary", "arbitrary"),
        name="ffn_gate_up",
    )(xg, ssq, wg, wu)


def _ffn_down_kernel(h_ref, w_ref, x_ref, o_ref, *, d_ff):
    k = pl.program_id(2)
    bk = h_ref.shape[1]
    n_steps = -(-d_ff // bk)
    last = d_ff - (n_steps - 1) * bk

    def body(width, first):
        p = _dot(h_ref[:, :width], w_ref[:width, :])
        if first:
            o_ref[...] = x_ref[...] + p
        else:
            o_ref[...] += p

    if n_steps == 1:
        body(last, True)
        return
    pl.when(k == 0)(lambda: body(bk, True))
    if n_steps > 2:
        pl.when((k > 0) & (k < n_steps - 1))(lambda: body(bk, False))
    pl.when(k == n_steps - 1)(lambda: body(last, False))


def _ffn_down(h1, wd, x1):
    m, f = h1.shape
    n = wd.shape[1]
    bm = _pick(m, (1024, 512, 256, 128, 64))
    bn = _pick(n, (1024, 512, 256, 128))
    bk = 2816
    assert (f % bk) % 256 == 0, "the last contraction block must stay MXU-row aligned"
    return pl.pallas_call(
        functools.partial(_ffn_down_kernel, d_ff=f),
        grid=(m // bm, n // bn, pl.cdiv(f, bk)),
        in_specs=[
            pl.BlockSpec((bm, bk), lambda i, j, k: (i, k)),
            pl.BlockSpec((bk, bn), lambda i, j, k: (k, j)),
            pl.BlockSpec((bm, bn), lambda i, j, k: (i, j)),
        ],
        out_specs=pl.BlockSpec((bm, bn), lambda i, j, k: (i, j)),
        out_shape=jax.ShapeDtypeStruct((m, n), F32),
        compiler_params=_params("parallel", "parallel", "arbitrary"),
        name="ffn_down_residual",
    )(h1, wd, x1)


def kernel(x_prompt, x_sample, state_gla, g_mix, w_in, w_s, b_s, ln_g, ln_b, w_gate_up, b_gate,
           gla_norm_g, w_out, g_ffn, w_ffn_gate, w_ffn_up, w_ffn_down, g_final):
    depth = g_mix.shape[0]
    n_p, len_p, d_model = x_prompt.shape
    n_s, len_s, _ = x_sample.shape
    rows_p, rows_s = n_p * len_p, n_s * len_s
    _, _, heads_b, dk, dv = state_gla.shape
    a_width = ln_g.shape[-1]
    a_heads, sgu_chunk = w_s.shape[1], w_s.shape[2]
    key_dim, val_dim = heads_b * dk, heads_b * dv
    rank = w_gate_up.shape[1]
    d_ff = w_ffn_gate.shape[-1]
    main_cols = 2 * a_width + 2 * key_dim + 2 * val_dim
    col_q = 2 * a_width
    col_k = col_q + key_dim
    col_v = col_k + key_dim
    col_r = col_v + val_dim
    chunk_p = min(sgu_chunk, len_p)
    chunk_s = min(sgu_chunk, len_s)

    x_p = x_prompt.reshape(rows_p, d_model)
    x_s = x_sample.reshape(rows_s, d_model)
    zero_state = jnp.zeros((n_p, heads_b, dk, dv), state_gla.dtype)

    gla_p, gla_s, vn_s = [], [], []
    for d in range(depth):
        w_up = jnp.pad(w_gate_up[d], ((0, LANES - rank), (0, 0))).astype(BF16)
        w_o = w_out[d].astype(BF16)
        wd = w_ffn_down[d].astype(BF16)
        head_dim = a_width // a_heads
        bias_full = jnp.repeat(b_s[d].T, head_dim, axis=1)
        row = lambda v: v.reshape(1, -1)

        w_in_t = w_in[d].T
        h, b_all = _prep(x_p, x_s, row(g_mix[d]), w_in_t, main_cols, rank, w_up, row(b_gate[d]))
        q_scale = dk ** -0.5
        proj = _in_proj(h, w_in_t, main_cols, (
            (0, col_q, _gelu_tanh),
            (col_q, col_k, lambda t: t * q_scale),
            (col_k, col_r, lambda t: t),
            (col_r, main_cols, _silu),
        ))

        cols = dict(col_q=col_q, col_k=col_k, col_v=col_v, col_r=col_r)
        shared = (row(gla_norm_g[d]),)
        ln = (row(ln_g[d]), row(ln_b[d]))
        a_p, o_p, s_p = _mixers(
            proj, b_all, zero_state, *shared, w_s[d][:, :chunk_p, :chunk_p], bias_full[:chunk_p],
            *ln, 0, n_p, len_p, **cols, sgu_chunk=chunk_p, emit_vn=False)
        a_s, o_s, s_s, vn = _mixers(
            proj, b_all, state_gla[d], *shared, w_s[d][:, :chunk_s, :chunk_s], bias_full[:chunk_s],
            *ln, rows_p, n_s, len_s, **cols, sgu_chunk=chunk_s, emit_vn=True)

        x1, xg, ssq = _mix(a_p, a_s, o_p, o_s, x_p, x_s, w_o, row(g_ffn[d]))
        h1 = _ffn_up(xg, ssq, w_ffn_gate[d], w_ffn_up[d])
        x = _ffn_down(h1, wd, x1)
        if d + 1 < depth:
            x_p, x_s = x[:rows_p], x[rows_p:]

        gla_p.append(s_p)
        gla_s.append(s_s)
        vn_s.append(vn.reshape(n_s, len_s, a_width))

    gf = g_final.reshape(1, -1)
    y_p = _rmsnorm(x, gf, F32, 0, rows_p).reshape(n_p, len_p, d_model)
    y_s = _rmsnorm(x, gf, F32, rows_p, rows_s).reshape(n_s, len_s, d_model)
    return (y_p, y_s, jnp.stack(gla_p), jnp.stack(gla_s), jnp.stack(vn_s))
```

```python
import functools
import math

import jax
import jax.numpy as jnp
from jax import lax
from jax.experimental import pallas as pl
from jax.experimental.pallas import tpu as pltpu

F32 = jnp.float32
BF16 = jnp.bfloat16

EPS = 1e-6
GLA_BLOCK = 64
GATE_TAU = 16.0
LANES = 128
V7X_VMEM_LIMIT_BYTES = 56 * 1024 * 1024


def _pick(n, candidates):
    for c in candidates:
        if n % c == 0:
            return c
    raise ValueError(f"no tile in {candidates} divides {n}")


def _params(*semantics):
    return pltpu.CompilerParams(dimension_semantics=semantics,
                                vmem_limit_bytes=V7X_VMEM_LIMIT_BYTES)


def _dot(a, b):
    return jnp.dot(a, b, preferred_element_type=F32)


def _gelu_tanh(x):
    c = math.sqrt(2.0 / math.pi)
    return 0.5 * x * (1.0 + jnp.tanh(c * (x + 0.044715 * (x * x * x))))


def _silu(x):
    return x * (1.0 / (1.0 + jnp.exp(-x)))


def _rms(x, g):
    return x * lax.rsqrt(jnp.mean(x * x, axis=-1, keepdims=True) + EPS) * g


def _prep_kernel(xp_ref, xs_ref, g_ref, win_ref, wup_ref, bg_ref, tri_ref, h_ref, b_ref, wlr_ref,
                 *, p_tiles, sub, rank):
    @pl.when(pl.program_id(0) == 0)
    def _():
        pad = jnp.zeros((LANES - rank, win_ref.shape[1]), F32)
        wlr_ref[...] = jnp.transpose(jnp.concatenate([win_ref[...], pad], axis=0)).astype(BF16)

    def body(x_ref):
        for c in range(x_ref.shape[0] // sub):
            rs = slice(c * sub, (c + 1) * sub)
            hb = _rms(x_ref[rs, :], g_ref[...]).astype(BF16)
            h_ref[rs, :] = hb
            g_lr = _dot(hb, wlr_ref[...])
            z = _dot(g_lr.astype(BF16), wup_ref[...]) + bg_ref[...]
            log_a = (jnp.minimum(z, 0.0) - jnp.log1p(jnp.exp(-jnp.abs(z)))) * (1.0 / GATE_TAU)
            hi = log_a.astype(BF16)
            r1 = log_a - hi.astype(F32)
            mid = r1.astype(BF16)
            lo = (r1 - mid.astype(F32)).astype(BF16)
            tri = tri_ref[...]
            b_ref[rs, :] = _dot(tri, hi) + _dot(tri, mid) + _dot(tri, lo)

    _on_prompt_or_sample(pl.program_id(0), p_tiles, body, (xp_ref,), (xs_ref,))


def _on_prompt_or_sample(i, p_tiles, body, prompt_refs, sample_refs):
    @pl.when(i < p_tiles)
    def _():
        body(*prompt_refs)

    @pl.when(i >= p_tiles)
    def _():
        body(*sample_refs)


def _split_specs(block, p_tiles, col_tiles=None, single_buffer_sample=False):
    def prompt_map(i, *g):
        col = jnp.where(i < p_tiles, g[0], col_tiles - 1) if col_tiles else 0
        return jnp.minimum(i, p_tiles - 1), col

    def sample_map(i, *g):
        col = jnp.where(i >= p_tiles, g[0], 0) if col_tiles else 0
        return jnp.maximum(i - p_tiles, 0), col

    mode = dict(pipeline_mode=pl.Buffered(1)) if single_buffer_sample else {}
    return [pl.BlockSpec(block, prompt_map), pl.BlockSpec(block, sample_map, **mode)]


def _prep(x_p, x_s, g, w_in_t, gate_row0, rank, w_up, b_gate):
    (n_p, d), n_s = x_p.shape, x_s.shape[0]
    n = n_p + n_s
    key_dim = w_up.shape[1]
    assert gate_row0 % rank == 0 and rank % 8 == 0 and rank <= LANES
    bm = _pick(math.gcd(n_p, n_s), (512, 256, 128, 64))
    sub = _pick(bm, (128, 64))
    idx = jnp.arange(sub)
    tri = ((idx[None, :] <= idx[:, None])
           & (idx[None, :] // GLA_BLOCK == idx[:, None] // GLA_BLOCK)).astype(BF16)
    p_tiles = n_p // bm
    const = lambda shape: pl.BlockSpec(shape, lambda i: (0, 0))
    return pl.pallas_call(
        functools.partial(_prep_kernel, p_tiles=p_tiles, sub=sub, rank=rank),
        grid=(n // bm,),
        in_specs=_split_specs((bm, d), p_tiles, single_buffer_sample=True) + [
            const((1, d)),
            pl.BlockSpec((rank, d), lambda i: (gate_row0 // rank, 0)),
            const(w_up.shape), const((1, key_dim)), const((sub, sub)),
        ],
        out_specs=[
            pl.BlockSpec((bm, d), lambda i: (i, 0)),
            pl.BlockSpec((bm, key_dim), lambda i: (i, 0)),
        ],
        out_shape=[
            jax.ShapeDtypeStruct((n, d), BF16),
            jax.ShapeDtypeStruct((n, key_dim), F32),
        ],
        scratch_shapes=[pltpu.VMEM((d, LANES), BF16)],
        compiler_params=_params("arbitrary"),
        name="prep_rmsnorm_gate",
    )(x_p, x_s, g, w_in_t, w_up, b_gate, tri)


class _WeightStream:
    def __init__(self, w_hbms, wbuf, stage, sem, *, bn, n_col_steps, first_col, first_width,
                 next_col, transposed=False):
        self.w_hbms, self.wbuf, self.stage, self.sem = w_hbms, wbuf, stage, sem
        self.transposed = transposed
        self.bn = bn
        self.rows = stage.shape[2]
        self.tile_rows = wbuf.shape[2] - self.rows
        self.n_chunks = self.tile_rows // self.rows
        self.n_col_steps = n_col_steps
        self.first_col, self.first_width = first_col, first_width
        self.next_col = next_col
        self.j = pl.program_id(0)
        self.i = pl.program_id(1)

    def _copy(self, m, chunk, col0, width, slot):
        if self.transposed:
            assert width == self.bn
            src = self.w_hbms[m].at[pl.ds(col0 + chunk * self.rows, self.rows), :]
            dst = self.stage.at[m, slot]
        else:
            src = self.w_hbms[m].at[pl.ds(chunk * self.rows, self.rows), pl.ds(col0, width)]
            dst = self.stage.at[m, slot, :, pl.ds(0, width)]
        return pltpu.make_async_copy(src, dst, self.sem.at[m, slot])

    def _next_copy(self, m, chunk):
        col0 = pl.multiple_of(self.next_col(self.j) * self.bn, self.bn)
        return self._copy(m, chunk, col0, self.bn, chunk % 2)

    def advance(self):
        j, i, n_mats = self.j, self.i, len(self.w_hbms)

        @pl.when((j == 0) & (i == 0))
        def _():
            self.stage[...] = jnp.zeros_like(self.stage)
            for c in range(self.n_chunks):
                for m in range(n_mats):
                    cp = self._copy(m, c, self.first_col * self.bn, self.first_width, c % 2)
                    cp.start()
                    cp.wait()
                    self._store_chunk(m, 0, c, c % 2)

        has_next = j + 1 < self.n_col_steps

        @pl.when(has_next & (i >= 1) & (i <= self.n_chunks))
        def _():
            for m in range(n_mats):
                self._next_copy(m, i - 1).wait()

        @pl.when(has_next & (i < self.n_chunks))
        def _():
            for m in range(n_mats):
                self._next_copy(m, i).start()

    def _store_chunk(self, m, tile_slot, chunk, stage_slot):
        row0 = chunk * self.rows
        if not isinstance(row0, int):
            row0 = pl.multiple_of(row0, self.rows)
        self.wbuf[m, tile_slot, pl.ds(row0, self.rows), :] = self.stage[m, stage_slot].astype(BF16)

    def cast_arrived_chunk(self):
        j, i = self.j, self.i
        valid = (j + 1 < self.n_col_steps) & (i >= 1) & (i <= self.n_chunks)
        chunk = jnp.where(valid, i - 1, self.n_chunks)
        for m in range(len(self.w_hbms)):
            self._store_chunk(m, (j + 1) % 2, chunk, (i + 1) % 2)

    def tile(self, m, width=None):
        w = self.wbuf[m, self.j % 2, :self.tile_rows, :]
        return w if width is None else w[:, :width]


class _CastRider:
    def __init__(self, src_hbm, dst_hbm, in_buf, out_buf, sem, step):
        self.src, self.dst, self.in_buf, self.out_buf, self.sem = src_hbm, dst_hbm, in_buf, out_buf, sem
        self.rows = in_buf.shape[1]
        self.n_chunks = src_hbm.shape[0] // self.rows
        self.step = step

    def _fetch(self, chunk):
        rs = pl.ds(pl.multiple_of(chunk * self.rows, self.rows), self.rows)
        return pltpu.make_async_copy(self.src.at[rs, :], self.in_buf.at[chunk % 2],
                                     self.sem.at[0, chunk % 2])

    def _write_back(self, chunk):
        rs = pl.ds(pl.multiple_of(chunk * self.rows, self.rows), self.rows)
        return pltpu.make_async_copy(self.out_buf.at[chunk % 2], self.dst.at[rs, :],
                                     self.sem.at[1, chunk % 2])

    def advance(self):
        g, n = self.step, self.n_chunks

        @pl.when(g == 0)
        def _():
            self.in_buf[...] = jnp.zeros_like(self.in_buf)

        steady = (g >= 3) & (g < n)

        @pl.when(steady)
        def _():
            self._write_back(g - 3).wait()
            self._fetch(g - 1).wait()
            self._fetch(g).start()
            self._write_back(g - 2).start()

        @pl.when(jnp.logical_not(steady))
        def _():
            pl.when((g >= 3) & (g < n + 3))(lambda: self._write_back(g - 3).wait())
            pl.when((g >= 1) & (g < n + 1))(lambda: self._fetch(g - 1).wait())
            pl.when(g < n)(lambda: self._fetch(g).start())
            pl.when((g >= 2) & (g < n + 2))(lambda: self._write_back(g - 2).start())

    def cast_arrived_chunk(self):
        slot = (self.step + 1) % 2
        self.out_buf[slot] = self.in_buf[slot].astype(BF16)


def _cast_rider_scratch(src_shape, n_steps):
    n_rows, n_cols = src_shape
    rows = 16
    while n_rows % rows or n_rows // rows + 3 > n_steps:
        rows += 16
        assert rows <= n_rows, "too few grid steps to ride the cast on"
    return [
        pltpu.VMEM((2, rows, n_cols), F32),
        pltpu.VMEM((2, rows, n_cols), BF16),
        pltpu.SemaphoreType.DMA((2, 2)),
    ]


def _weight_stream_scratch(n_mats, k, bn, n_row_steps, transposed=False):
    assert n_row_steps >= 2, "weight chunks are prefetched across the row steps of one column tile"
    tile_rows, tile_cols = (bn, k) if transposed else (k, bn)
    n_chunks = 1
    while n_chunks * 2 <= min(n_row_steps - 1, 16) and tile_rows % (n_chunks * 2 * 16) == 0:
        n_chunks *= 2
    rows = tile_rows // n_chunks
    return [
        pltpu.VMEM((n_mats, 2, tile_rows + rows, tile_cols), BF16),
        pltpu.VMEM((n_mats, 2, rows, tile_cols), F32),
        pltpu.SemaphoreType.DMA((n_mats, 2)),
    ]


def _in_proj_kernel(a_ref, wt_hbm, ride_hbm, o_ref, rode_hbm, wbuf, stage, sem,
                    ride_in, ride_out, ride_sem, *, epilogues, n_col_steps):
    bn = o_ref.shape[1]
    ws = _WeightStream([wt_hbm], wbuf, stage, sem, bn=bn, n_col_steps=n_col_steps,
                       first_col=0, first_width=bn, next_col=lambda j: j + 1, transposed=True)
    rider = _CastRider(ride_hbm, rode_hbm, ride_in, ride_out, ride_sem,
                       ws.j * pl.num_programs(1) + ws.i)
    ws.advance()
    rider.advance()
    for first, end, fn in epilogues:
        @pl.when((ws.j >= first) & (ws.j < end))
        def _(fn=fn):
            ws.cast_arrived_chunk()
            rider.cast_arrived_chunk()
            acc = lax.dot_general(a_ref[...], ws.tile(0), (((1,), (1,)), ((), ())),
                                  preferred_element_type=F32)
            o_ref[...] = fn(acc).astype(o_ref.dtype)


def _in_proj(h, w_t, n_cols, col_epilogues, ride):
    m, k = h.shape
    bm = _pick(m, (1024, 512, 256, 128, 64))
    bn = _pick(math.gcd(*[end for _, end, _ in col_epilogues]), (1024, 512, 256, 128))
    epilogues = tuple((first // bn, end // bn, fn) for first, end, fn in col_epilogues)
    n_col_steps = n_cols // bn
    n_row_steps = m // bm
    anywhere = pl.BlockSpec(memory_space=pl.ANY)
    return pl.pallas_call(
        functools.partial(_in_proj_kernel, epilogues=epilogues, n_col_steps=n_col_steps),
        grid=(n_col_steps, n_row_steps),
        in_specs=[pl.BlockSpec((bm, k), lambda j, i: (i, 0)), anywhere, anywhere],
        out_specs=[pl.BlockSpec((bm, bn), lambda j, i: (i, j)), anywhere],
        out_shape=[jax.ShapeDtypeStruct((m, n_cols), BF16),
                   jax.ShapeDtypeStruct(ride.shape, BF16)],
        scratch_shapes=(_weight_stream_scratch(1, k, bn, n_row_steps, transposed=True)
                        + _cast_rider_scratch(ride.shape, n_col_steps * n_row_steps)),
        compiler_params=_params("arbitrary", "arbitrary"),
        name="gemm_in_proj",
    )(h, w_t, ride)


def _sgu_tile(u_ref, v_ref, wm_ref, bias_ref, lng_ref, lnb_ref, o_ref, vn_ref):
    rows, width = u_ref.shape
    heads, chunk, _ = wm_ref.shape
    head_dim = width // heads
    n_chunks = rows // chunk
    v = v_ref[...].astype(F32)
    mu = jnp.mean(v, axis=-1, keepdims=True)
    vc = v - mu
    var = jnp.mean(vc * vc, axis=-1, keepdims=True)
    vn = vc * lax.rsqrt(var + EPS) * lng_ref[...] + lnb_ref[...]
    if vn_ref is not None:
        vn_ref[...] = vn
    vnb = vn.astype(BF16)
    for h in range(heads):
        lo, hi = h * head_dim, (h + 1) * head_dim
        rhs = jnp.concatenate(
            [vnb[c * chunk:(c + 1) * chunk, lo:hi] for c in range(n_chunks)], axis=1)
        mixed = _dot(wm_ref[h], rhs)
        bias = bias_ref[:, lo:hi]
        for c in range(n_chunks):
            rs = slice(c * chunk, (c + 1) * chunk)
            u = u_ref[rs, lo:hi].astype(F32)
            m = mixed[:, c * head_dim:(c + 1) * head_dim] + bias
            o_ref[rs, lo:hi] = (u * m).astype(o_ref.dtype)


def _mixers_kernel(u_ref, vg_ref, q_ref, k_ref, v_ref, r_ref, b_ref, s0_ref, gn_ref,
                   w_ref, bias_ref, lng_ref, lnb_ref, a_ref, o_ref, s_out_ref, *rest,
                   heads, dk, dv, blocks, emit_vn):
    vn_ref, s_ref, wm_ref = rest if emit_vn else (None,) + rest
    t = pl.program_id(1)

    @pl.when((pl.program_id(0) == 0) & (t == 0))
    def _():
        chunk = w_ref.shape[1]
        ri = lax.broadcasted_iota(jnp.int32, (chunk, chunk), 0)
        ci = lax.broadcasted_iota(jnp.int32, (chunk, chunk), 1)
        for h in range(w_ref.shape[0]):
            wm_ref[h] = jnp.where(ci <= ri, w_ref[h], 0.0).astype(BF16)

    @pl.when(t == 0)
    def _():
        s_ref[...] = s0_ref[0]

    _sgu_tile(u_ref, vg_ref, wm_ref, bias_ref, lng_ref, lnb_ref, a_ref, vn_ref)

    ri = lax.broadcasted_iota(jnp.int32, (GLA_BLOCK, GLA_BLOCK), 0)
    ci = lax.broadcasted_iota(jnp.int32, (GLA_BLOCK, GLA_BLOCK), 1)
    causal = ci <= ri
    half = GLA_BLOCK // 2
    for blk in range(blocks):
        rs = slice(blk * GLA_BLOCK, (blk + 1) * GLA_BLOCK)
        for h in range(heads):
            ks = slice(h * dk, (h + 1) * dk)
            vs = slice(h * dv, (h + 1) * dv)
            q = q_ref[rs, ks].astype(F32)
            k = k_ref[rs, ks].astype(F32)
            v = v_ref[rs, vs]
            b = b_ref[rs, ks]
            b_mid = b[half:half + 1, :]
            b_last = b[GLA_BLOCK - 1:GLA_BLOCK, :]
            q_in = (q * jnp.exp(b - b_mid)).astype(BF16)
            k_in = (k * jnp.exp(b_mid - b)).astype(BF16)
            att = lax.dot_general(q_in, k_in, (((1,), (1,)), ((), ())),
                                  preferred_element_type=F32)
            att = jnp.where(causal, att, 0.0).astype(BF16)
            s = s_ref[h]
            o = _dot(att, v) + _dot((q * jnp.exp(b)).astype(BF16), s.astype(BF16))
            k_out = (k * jnp.exp(b_last - b)).astype(BF16)
            kv = lax.dot_general(k_out, v, (((0,), (0,)), ((), ())),
                                 preferred_element_type=F32)
            decay_col = jnp.transpose(
                jnp.broadcast_to(jnp.exp(b_last), (LANES, dk)))
            decay = jnp.concatenate([decay_col] * (dv // LANES), axis=1)
            s_ref[h] = decay * s + kv
            o_n = _rms(o, gn_ref[...])
            o_ref[rs, vs] = (o_n * r_ref[rs, vs].astype(F32)).astype(o_ref.dtype)

    @pl.when(t == pl.num_programs(1) - 1)
    def _():
        s_out_ref[0] = s_ref[...]


def _mixers(proj, b_all, s0, gn, w_s, bias_full, ln_g, ln_b, row0, n_streams, stream_len, *,
            col_q, col_k, col_v, col_r, sgu_chunk, emit_vn):
    _, heads, dk, dv = s0.shape
    kd, vd = heads * dk, heads * dv
    a_width = ln_g.shape[1]
    blocks = _pick(stream_len // GLA_BLOCK, (4, 2, 1))
    rows = blocks * GLA_BLOCK
    assert rows % sgu_chunk == 0, "an SGU chunk may not straddle two row tiles"
    steps = stream_len // rows
    r0 = row0 // rows
    cq, ck, cv, cr = col_q // kd, col_k // kd, col_v // vd, col_r // vd
    assert w_s.shape[1:] == (sgu_chunk, sgu_chunk)
    kern = functools.partial(_mixers_kernel, heads=heads, dk=dk, dv=dv, blocks=blocks,
                             emit_vn=emit_vn)
    row = lambda s, t: r0 + s * steps + t
    out_row = lambda s, t: s * steps + t
    n_rows = n_streams * stream_len
    const = lambda shape: pl.BlockSpec(shape, lambda s, t: (0,) * len(shape))
    out_specs = [
        pl.BlockSpec((rows, a_width), lambda s, t: (out_row(s, t), 0)),
        pl.BlockSpec((rows, vd), lambda s, t: (out_row(s, t), 0)),
        pl.BlockSpec((1, heads, dk, dv), lambda s, t: (s, 0, 0, 0)),
    ]
    out_shape = [
        jax.ShapeDtypeStruct((n_rows, a_width), BF16),
        jax.ShapeDtypeStruct((n_rows, vd), BF16),
        jax.ShapeDtypeStruct(s0.shape, F32),
    ]
    if emit_vn:
        out_specs.append(pl.BlockSpec((rows, a_width), lambda s, t: (out_row(s, t), 0)))
        out_shape.append(jax.ShapeDtypeStruct((n_rows, a_width), F32))
    return pl.pallas_call(
        kern,
        grid=(n_streams, steps),
        in_specs=[
            pl.BlockSpec((rows, a_width), lambda s, t: (row(s, t), 0)),
            pl.BlockSpec((rows, a_width), lambda s, t: (row(s, t), 1)),
            pl.BlockSpec((rows, kd), lambda s, t: (row(s, t), cq)),
            pl.BlockSpec((rows, kd), lambda s, t: (row(s, t), ck)),
            pl.BlockSpec((rows, vd), lambda s, t: (row(s, t), cv)),
            pl.BlockSpec((rows, vd), lambda s, t: (row(s, t), cr)),
            pl.BlockSpec((rows, kd), lambda s, t: (row(s, t), 0)),
            pl.BlockSpec((1, heads, dk, dv), lambda s, t: (s, 0, 0, 0)),
            const((1, dv)), const(w_s.shape), const(bias_full.shape),
            const((1, a_width)), const((1, a_width)),
        ],
        out_specs=out_specs,
        out_shape=out_shape,
        scratch_shapes=[pltpu.VMEM((heads, dk, dv), F32), pltpu.VMEM(w_s.shape, BF16)],
        compiler_params=_params("arbitrary", "arbitrary"),
        name="sgu_gla_mixers",
    )(proj, proj, proj, proj, proj, proj, b_all, s0, gn, w_s, bias_full, ln_g, ln_b)


def _mix_kernel(ap_ref, as_ref, bp_ref, bs_ref, xp_ref, xs_ref, wa_ref, wb_ref, g_ref,
                x1_ref, xg_ref, ssq_ref, *, p_tiles):
    j = pl.program_id(1)

    def body(a_ref, b_ref, x_ref):
        half = x_ref.shape[0] // 2
        for rs in (slice(0, half), slice(half, 2 * half)):
            x1 = x_ref[rs, :] + (_dot(a_ref[rs, :], wa_ref[...]) + _dot(b_ref[rs, :], wb_ref[...]))
            x1_ref[rs, :] = x1
            xg_ref[rs, :] = (x1 * g_ref[...]).astype(xg_ref.dtype)
            sq = x1 * x1
            part = sq[:, :LANES]
            for c in range(1, sq.shape[1] // LANES):
                part = part + sq[:, c * LANES:(c + 1) * LANES]
            ssq_ref[rs, :] += part

    @pl.when(j == 0)
    def _():
        ssq_ref[...] = jnp.zeros_like(ssq_ref)

    _on_prompt_or_sample(pl.program_id(0), p_tiles, body,
                         (ap_ref, bp_ref, xp_ref), (as_ref, bs_ref, xs_ref))


def _mix(a_p, a_s, b_p, b_s, x_p, x_s, w_out, g):
    (n_p, ka), n_s = a_p.shape, a_s.shape[0]
    m = n_p + n_s
    kb = b_p.shape[1]
    n = w_out.shape[1]
    assert ka == kb, "head groups of different widths need separate weight specs"
    bm = _pick(math.gcd(n_p, n_s), (1024, 512, 256, 128, 64))
    bn = _pick(n, (512, 256, 128))
    p_tiles = n_p // bm
    split = functools.partial(_split_specs, p_tiles=p_tiles, single_buffer_sample=True)
    return pl.pallas_call(
        functools.partial(_mix_kernel, p_tiles=p_tiles),
        grid=(m // bm, n // bn),
        in_specs=(split((bm, ka)) + split((bm, kb)) + split((bm, bn), col_tiles=n // bn) + [
            pl.BlockSpec((ka, bn), lambda i, j: (0, j)),
            pl.BlockSpec((kb, bn), lambda i, j: (1, j)),
            pl.BlockSpec((1, bn), lambda i, j: (0, j)),
        ]),
        out_specs=[
            pl.BlockSpec((bm, bn), lambda i, j: (i, j)),
            pl.BlockSpec((bm, bn), lambda i, j: (i, j)),
            pl.BlockSpec((bm, LANES), lambda i, j: (i, 0)),
        ],
        out_shape=[
            jax.ShapeDtypeStruct((m, n), F32),
            jax.ShapeDtypeStruct((m, n), BF16),
            jax.ShapeDtypeStruct((m, LANES), F32),
        ],
        compiler_params=_params("parallel", "arbitrary"),
        name="gemm_out_proj_residual",
    )(a_p, a_s, b_p, b_s, x_p, x_s, w_out, w_out, g)


def _rmsnorm_kernel(x_ref, g_ref, o_ref):
    o_ref[...] = _rms(x_ref[...], g_ref[...]).astype(o_ref.dtype)


def _rmsnorm(x, g, out_dtype, row0=0, n_rows=None):
    n, d = x.shape
    n_rows = n if n_rows is None else n_rows
    bm = _pick(math.gcd(n_rows, row0) if row0 else n_rows, (256, 128, 64))
    r0 = row0 // bm
    return pl.pallas_call(
        _rmsnorm_kernel,
        grid=(n_rows // bm,),
        in_specs=[
            pl.BlockSpec((bm, d), lambda i: (r0 + i, 0)),
            pl.BlockSpec((1, d), lambda i: (0, 0)),
        ],
        out_specs=pl.BlockSpec((bm, d), lambda i: (i, 0)),
        out_shape=jax.ShapeDtypeStruct((n_rows, d), out_dtype),
        compiler_params=_params("parallel"),
        name="rmsnorm",
    )(x, g)


def _ffn_up_kernel(xg_ref, ssq_ref, wg_hbm, wu_hbm, ride_hbm, o_ref, rode_hbm, wbuf, stage, sem,
                   ride_in, ride_out, ride_sem, *, d_model, d_ff):
    bn = o_ref.shape[1]
    n_steps = -(-d_ff // bn)
    edge = d_ff - (n_steps - 1) * bn
    edge_first = edge != bn
    ws = _WeightStream([wg_hbm, wu_hbm], wbuf, stage, sem, bn=bn, n_col_steps=n_steps,
                       first_col=n_steps - 1 if edge_first else 0, first_width=edge,
                       next_col=(lambda j: j) if edge_first else (lambda j: j + 1))
    rider = _CastRider(ride_hbm, rode_hbm, ride_in, ride_out, ride_sem,
                       ws.j * pl.num_programs(1) + ws.i)
    ws.advance()
    rider.advance()

    def body(width):
        ws.cast_arrived_chunk()
        rider.cast_arrived_chunk()
        rinv = lax.rsqrt(jnp.sum(ssq_ref[...], axis=-1, keepdims=True) * (1.0 / d_model) + EPS)
        h = xg_ref[...]
        g = _dot(h, ws.tile(0, width)) * rinv
        u = _dot(h, ws.tile(1, width)) * rinv
        o_ref[:, :width] = (_silu(g) * u).astype(o_ref.dtype)

    if edge_first:
        pl.when(ws.j == 0)(lambda: body(edge))
        pl.when(ws.j > 0)(lambda: body(bn))
    else:
        body(bn)


def _ffn_up(xg, ssq, wg, wu, ride):
    m, k = xg.shape
    f = wg.shape[1]
    bm = _pick(m, (1024, 512, 256, 128, 64))
    bn = 512
    assert (f % bn) % 256 == 0, "the edge column block must stay MXU-column aligned"
    n_steps = pl.cdiv(f, bn)
    n_row_steps = m // bm
    col = (lambda j: j) if f % bn == 0 else (lambda j: jnp.where(j == 0, n_steps - 1, j - 1))
    anywhere = pl.BlockSpec(memory_space=pl.ANY)
    return pl.pallas_call(
        functools.partial(_ffn_up_kernel, d_model=k, d_ff=f),
        grid=(n_steps, n_row_steps),
        in_specs=[
            pl.BlockSpec((bm, k), lambda j, i: (i, 0)),
            pl.BlockSpec((bm, LANES), lambda j, i: (i, 0)),
            anywhere, anywhere, anywhere,
        ],
        out_specs=[pl.BlockSpec((bm, bn), lambda j, i: (i, col(j))), anywhere],
        out_shape=[jax.ShapeDtypeStruct((m, f), BF16), jax.ShapeDtypeStruct(ride.shape, BF16)],
        scratch_shapes=(_weight_stream_scratch(2, k, bn, n_row_steps)
                        + _cast_rider_scratch(ride.shape, n_steps * n_row_steps)),
        compiler_params=_params("arbitrary", "arbitrary"),
        name="ffn_gate_up",
    )(xg, ssq, wg, wu, ride)


def _ffn_down_kernel(h_ref, w_ref, x_ref, o_ref, *, d_ff):
    k = pl.program_id(2)
    bk = h_ref.shape[1]
    n_steps = -(-d_ff // bk)
    last = d_ff - (n_steps - 1) * bk

    def body(width, first):
        p = _dot(h_ref[:, :width], w_ref[:width, :])
        if first:
            o_ref[...] = x_ref[...] + p
        else:
            o_ref[...] += p

    if n_steps == 1:
        body(last, True)
        return
    pl.when(k == 0)(lambda: body(bk, True))
    if n_steps > 2:
        pl.when((k > 0) & (k < n_steps - 1))(lambda: body(bk, False))
    pl.when(k == n_steps - 1)(lambda: body(last, False))


def _ffn_down(h1, wd, x1):
    m, f = h1.shape
    n = wd.shape[1]
    bm = _pick(m, (1024, 512, 256, 128, 64))
    bn = _pick(n, (1024, 512, 256, 128))
    bk = 2816
    assert (f % bk) % 256 == 0, "the last contraction block must stay MXU-row aligned"
    return pl.pallas_call(
        functools.partial(_ffn_down_kernel, d_ff=f),
        grid=(m // bm, n // bn, pl.cdiv(f, bk)),
        in_specs=[
            pl.BlockSpec((bm, bk), lambda i, j, k: (i, k)),
            pl.BlockSpec((bk, bn), lambda i, j, k: (k, j)),
            pl.BlockSpec((bm, bn), lambda i, j, k: (i, j)),
        ],
        out_specs=pl.BlockSpec((bm, bn), lambda i, j, k: (i, j)),
        out_shape=jax.ShapeDtypeStruct((m, n), F32),
        compiler_params=_params("parallel", "parallel", "arbitrary"),
        name="ffn_down_residual",
    )(h1, wd, x1)


def kernel(x_prompt, x_sample, state_gla, g_mix, w_in, w_s, b_s, ln_g, ln_b, w_gate_up, b_gate,
           gla_norm_g, w_out, g_ffn, w_ffn_gate, w_ffn_up, w_ffn_down, g_final):
    depth = g_mix.shape[0]
    n_p, len_p, d_model = x_prompt.shape
    n_s, len_s, _ = x_sample.shape
    rows_p, rows_s = n_p * len_p, n_s * len_s
    _, _, heads_b, dk, dv = state_gla.shape
    a_width = ln_g.shape[-1]
    a_heads, sgu_chunk = w_s.shape[1], w_s.shape[2]
    key_dim, val_dim = heads_b * dk, heads_b * dv
    rank = w_gate_up.shape[1]
    d_ff = w_ffn_gate.shape[-1]
    main_cols = 2 * a_width + 2 * key_dim + 2 * val_dim
    col_q = 2 * a_width
    col_k = col_q + key_dim
    col_v = col_k + key_dim
    col_r = col_v + val_dim
    chunk_p = min(sgu_chunk, len_p)
    chunk_s = min(sgu_chunk, len_s)

    x_p = x_prompt.reshape(rows_p, d_model)
    x_s = x_sample.reshape(rows_s, d_model)
    zero_state = jnp.zeros((n_p, heads_b, dk, dv), state_gla.dtype)

    gla_p, gla_s, vn_s = [], [], []
    for d in range(depth):
        w_up = jnp.pad(w_gate_up[d], ((0, LANES - rank), (0, 0))).astype(BF16)
        head_dim = a_width // a_heads
        bias_full = jnp.repeat(b_s[d].T, head_dim, axis=1)
        row = lambda v: v.reshape(1, -1)

        w_in_t = w_in[d].T
        h, b_all = _prep(x_p, x_s, row(g_mix[d]), w_in_t, main_cols, rank, w_up, row(b_gate[d]))
        q_scale = dk ** -0.5
        proj, w_o = _in_proj(h, w_in_t, main_cols, (
            (0, col_q, _gelu_tanh),
            (col_q, col_k, lambda t: t * q_scale),
            (col_k, col_r, lambda t: t),
            (col_r, main_cols, _silu),
        ), ride=w_out[d])

        cols = dict(col_q=col_q, col_k=col_k, col_v=col_v, col_r=col_r)
        shared = (row(gla_norm_g[d]),)
        ln = (row(ln_g[d]), row(ln_b[d]))
        a_p, o_p, s_p = _mixers(
            proj, b_all, zero_state, *shared, w_s[d][:, :chunk_p, :chunk_p], bias_full[:chunk_p],
            *ln, 0, n_p, len_p, **cols, sgu_chunk=chunk_p, emit_vn=False)
        a_s, o_s, s_s, vn = _mixers(
            proj, b_all, state_gla[d], *shared, w_s[d][:, :chunk_s, :chunk_s], bias_full[:chunk_s],
            *ln, rows_p, n_s, len_s, **cols, sgu_chunk=chunk_s, emit_vn=True)

        x1, xg, ssq = _mix(a_p, a_s, o_p, o_s, x_p, x_s, w_o, row(g_ffn[d]))
        h1, wd = _ffn_up(xg, ssq, w_ffn_gate[d], w_ffn_up[d], ride=w_ffn_down[d])
        x = _ffn_down(h1, wd, x1)
        if d + 1 < depth:
            x_p, x_s = x[:rows_p], x[rows_p:]

        gla_p.append(s_p)
        gla_s.append(s_s)
        vn_s.append(vn.reshape(n_s, len_s, a_width))

    gf = g_final.reshape(1, -1)
    y_p = _rmsnorm(x, gf, F32, 0, rows_p).reshape(n_p, len_p, d_model)
    y_s = _rmsnorm(x, gf, F32, rows_p, rows_s).reshape(n_s, len_s, d_model)
    return (y_p, y_s, jnp.stack(gla_p), jnp.stack(gla_s), jnp.stack(vn_s))
```

```python
import functools
import math

import jax
import jax.numpy as jnp
from jax import lax
from jax.experimental import pallas as pl
from jax.experimental.pallas import tpu as pltpu

F32 = jnp.float32
BF16 = jnp.bfloat16

EPS = 1e-6
GLA_BLOCK = 64
GATE_TAU = 16.0
LANES = 128
BF16_SUBLANES = 16
V7X_MXU_DIM = 256
V7X_VMEM_LIMIT_BYTES = 56 * 1024 * 1024


def _pick(n, candidates):
    for c in candidates:
        if n % c == 0:
            return c
    raise ValueError(f"no tile in {candidates} divides {n}")


def _params(*semantics):
    return pltpu.CompilerParams(dimension_semantics=semantics,
                                vmem_limit_bytes=V7X_VMEM_LIMIT_BYTES)


def _dot(a, b):
    return jnp.dot(a, b, preferred_element_type=F32)


def _gelu_tanh(x):
    c = math.sqrt(2.0 / math.pi)
    return 0.5 * x * (1.0 + jnp.tanh(c * (x + 0.044715 * (x * x * x))))


def _silu(x):
    return x * (1.0 / (1.0 + jnp.exp(-x)))


def _rms(x, g):
    return x * lax.rsqrt(jnp.mean(x * x, axis=-1, keepdims=True) + EPS) * g


def _prep_kernel(xp_ref, xs_ref, g_ref, win_ref, wup_ref, bg_ref, tri_ref, h_ref, b_ref, wlr_ref,
                 *, p_tiles, sub, rank):
    @pl.when(pl.program_id(0) == 0)
    def _():
        pad = jnp.zeros((LANES - rank, win_ref.shape[1]), F32)
        wlr_ref[...] = jnp.transpose(jnp.concatenate([win_ref[...], pad], axis=0)).astype(BF16)

    def body(x_ref):
        for c in range(x_ref.shape[0] // sub):
            rs = slice(c * sub, (c + 1) * sub)
            hb = _rms(x_ref[rs, :], g_ref[...]).astype(BF16)
            h_ref[rs, :] = hb
            g_lr = _dot(hb, wlr_ref[...])
            z = _dot(g_lr.astype(BF16), wup_ref[...]) + bg_ref[...]
            log_a = (jnp.minimum(z, 0.0) - jnp.log1p(jnp.exp(-jnp.abs(z)))) * (1.0 / GATE_TAU)
            hi = log_a.astype(BF16)
            r1 = log_a - hi.astype(F32)
            mid = r1.astype(BF16)
            lo = (r1 - mid.astype(F32)).astype(BF16)
            tri = tri_ref[...]
            b_ref[rs, :] = _dot(tri, hi) + _dot(tri, mid) + _dot(tri, lo)

    _on_prompt_or_sample(pl.program_id(0), p_tiles, body, (xp_ref,), (xs_ref,))


def _on_prompt_or_sample(i, p_tiles, body, prompt_refs, sample_refs):
    @pl.when(i < p_tiles)
    def _():
        body(*prompt_refs)

    @pl.when(i >= p_tiles)
    def _():
        body(*sample_refs)


def _split_specs(block, p_tiles, col_tiles=None, single_buffer_sample=False):
    def prompt_map(i, *g):
        col = jnp.where(i < p_tiles, g[0], col_tiles - 1) if col_tiles else 0
        return jnp.minimum(i, p_tiles - 1), col

    def sample_map(i, *g):
        col = jnp.where(i >= p_tiles, g[0], 0) if col_tiles else 0
        return jnp.maximum(i - p_tiles, 0), col

    mode = dict(pipeline_mode=pl.Buffered(1)) if single_buffer_sample else {}
    return [pl.BlockSpec(block, prompt_map), pl.BlockSpec(block, sample_map, **mode)]


def _prep(x_p, x_s, g, w_in_t, gate_row0, rank, w_up, b_gate):
    (n_p, d), n_s = x_p.shape, x_s.shape[0]
    n = n_p + n_s
    key_dim = w_up.shape[1]
    assert gate_row0 % rank == 0 and rank % 8 == 0 and rank <= LANES
    bm = _pick(math.gcd(n_p, n_s), (512, 256, 128, 64))
    sub = _pick(bm, (128, 64))
    idx = jnp.arange(sub)
    tri = ((idx[None, :] <= idx[:, None])
           & (idx[None, :] // GLA_BLOCK == idx[:, None] // GLA_BLOCK)).astype(BF16)
    p_tiles = n_p // bm
    const = lambda shape: pl.BlockSpec(shape, lambda i: (0, 0))
    return pl.pallas_call(
        functools.partial(_prep_kernel, p_tiles=p_tiles, sub=sub, rank=rank),
        grid=(n // bm,),
        in_specs=_split_specs((bm, d), p_tiles, single_buffer_sample=True) + [
            const((1, d)),
            pl.BlockSpec((rank, d), lambda i: (gate_row0 // rank, 0)),
            const(w_up.shape), const((1, key_dim)), const((sub, sub)),
        ],
        out_specs=[
            pl.BlockSpec((bm, d), lambda i: (i, 0)),
            pl.BlockSpec((bm, key_dim), lambda i: (i, 0)),
        ],
        out_shape=[
            jax.ShapeDtypeStruct((n, d), BF16),
            jax.ShapeDtypeStruct((n, key_dim), F32),
        ],
        scratch_shapes=[pltpu.VMEM((d, LANES), BF16)],
        compiler_params=_params("arbitrary"),
        name="prep_rmsnorm_gate",
    )(x_p, x_s, g, w_in_t, w_up, b_gate, tri)


class _WeightStream:
    def __init__(self, w_hbms, wbuf, stage, sem, *, bn, n_col_steps, first_col, first_width,
                 next_col, transposed=False):
        self.w_hbms, self.wbuf, self.stage, self.sem = w_hbms, wbuf, stage, sem
        self.transposed = transposed
        self.bn = bn
        self.rows = stage.shape[2]
        self.tile_rows = wbuf.shape[2] - self.rows
        self.n_chunks = self.tile_rows // self.rows
        self.n_col_steps = n_col_steps
        self.first_col, self.first_width = first_col, first_width
        self.next_col = next_col
        self.j = pl.program_id(0)
        self.i = pl.program_id(1)

    def _copy(self, m, chunk, col0, width, slot):
        if self.transposed:
            assert width == self.bn
            src = self.w_hbms[m].at[pl.ds(col0 + chunk * self.rows, self.rows), :]
            dst = self.stage.at[m, slot]
        else:
            src = self.w_hbms[m].at[pl.ds(chunk * self.rows, self.rows), pl.ds(col0, width)]
            dst = self.stage.at[m, slot, :, pl.ds(0, width)]
        return pltpu.make_async_copy(src, dst, self.sem.at[m, slot])

    def _next_copy(self, m, chunk):
        col0 = pl.multiple_of(self.next_col(self.j) * self.bn, self.bn)
        return self._copy(m, chunk, col0, self.bn, chunk % 2)

    def advance(self):
        j, i, n_mats = self.j, self.i, len(self.w_hbms)

        @pl.when((j == 0) & (i == 0))
        def _():
            self.stage[...] = jnp.zeros_like(self.stage)
            for c in range(self.n_chunks):
                for m in range(n_mats):
                    cp = self._copy(m, c, self.first_col * self.bn, self.first_width, c % 2)
                    cp.start()
                    cp.wait()
                    self._store_chunk(m, 0, c, c % 2)

        has_next = j + 1 < self.n_col_steps

        @pl.when(has_next & (i >= 1) & (i <= self.n_chunks))
        def _():
            for m in range(n_mats):
                self._next_copy(m, i - 1).wait()

        @pl.when(has_next & (i < self.n_chunks))
        def _():
            for m in range(n_mats):
                self._next_copy(m, i).start()

    def _store_chunk(self, m, tile_slot, chunk, stage_slot):
        row0 = chunk * self.rows
        if not isinstance(row0, int):
            row0 = pl.multiple_of(row0, self.rows)
        self.wbuf[m, tile_slot, pl.ds(row0, self.rows), :] = self.stage[m, stage_slot].astype(BF16)

    def cast_arrived_chunk(self):
        j, i = self.j, self.i
        valid = (j + 1 < self.n_col_steps) & (i >= 1) & (i <= self.n_chunks)
        chunk = jnp.where(valid, i - 1, self.n_chunks)
        for m in range(len(self.w_hbms)):
            self._store_chunk(m, (j + 1) % 2, chunk, (i + 1) % 2)

    def tile(self, m, width=None):
        w = self.wbuf[m, self.j % 2, :self.tile_rows, :]
        return w if width is None else w[:, :width]


class _CastRider:
    def __init__(self, src_hbm, dst_hbm, in_buf, out_buf, sem, step):
        self.src, self.dst, self.in_buf, self.out_buf, self.sem = src_hbm, dst_hbm, in_buf, out_buf, sem
        self.rows = in_buf.shape[1]
        self.n_chunks = src_hbm.shape[0] // self.rows
        self.step = step

    def _fetch(self, chunk):
        rs = pl.ds(pl.multiple_of(chunk * self.rows, self.rows), self.rows)
        return pltpu.make_async_copy(self.src.at[rs, :], self.in_buf.at[chunk % 2],
                                     self.sem.at[0, chunk % 2])

    def _write_back(self, chunk):
        rs = pl.ds(pl.multiple_of(chunk * self.rows, self.rows), self.rows)
        return pltpu.make_async_copy(self.out_buf.at[chunk % 2], self.dst.at[rs, :],
                                     self.sem.at[1, chunk % 2])

    def advance(self):
        g, n = self.step, self.n_chunks

        @pl.when(g == 0)
        def _():
            self.in_buf[...] = jnp.zeros_like(self.in_buf)

        steady = (g >= 3) & (g < n)

        @pl.when(steady)
        def _():
            self._write_back(g - 3).wait()
            self._fetch(g - 1).wait()
            self._fetch(g).start()
            self._write_back(g - 2).start()

        @pl.when(jnp.logical_not(steady))
        def _():
            pl.when((g >= 3) & (g < n + 3))(lambda: self._write_back(g - 3).wait())
            pl.when((g >= 1) & (g < n + 1))(lambda: self._fetch(g - 1).wait())
            pl.when(g < n)(lambda: self._fetch(g).start())
            pl.when((g >= 2) & (g < n + 2))(lambda: self._write_back(g - 2).start())

    def cast_arrived_chunk(self):
        slot = (self.step + 1) % 2
        self.out_buf[slot] = self.in_buf[slot].astype(BF16)


def _cast_rider_scratch(src_shape, n_steps):
    n_rows, n_cols = src_shape
    rows = BF16_SUBLANES
    while n_rows % rows or n_rows // rows + 3 > n_steps:
        rows += BF16_SUBLANES
        assert rows <= n_rows, "too few grid steps to ride the cast on"
    return [
        pltpu.VMEM((2, rows, n_cols), F32),
        pltpu.VMEM((2, rows, n_cols), BF16),
        pltpu.SemaphoreType.DMA((2, 2)),
    ]


_MAX_WEIGHT_CHUNKS = 16


def _weight_stream_scratch(n_mats, k, bn, n_row_steps, transposed=False):
    assert n_row_steps >= 2, "weight chunks are prefetched across the row steps of one column tile"
    tile_rows, tile_cols = (bn, k) if transposed else (k, bn)
    n_chunks = 1
    while (n_chunks * 2 <= min(n_row_steps - 1, _MAX_WEIGHT_CHUNKS)
           and tile_rows % (n_chunks * 2 * BF16_SUBLANES) == 0):
        n_chunks *= 2
    rows = tile_rows // n_chunks
    return [
        pltpu.VMEM((n_mats, 2, tile_rows + rows, tile_cols), BF16),
        pltpu.VMEM((n_mats, 2, rows, tile_cols), F32),
        pltpu.SemaphoreType.DMA((n_mats, 2)),
    ]


def _in_proj_kernel(a_ref, wt_hbm, ride_hbm, o_ref, rode_hbm, wbuf, stage, sem,
                    ride_in, ride_out, ride_sem, *, epilogues, n_col_steps):
    bn = o_ref.shape[1]
    ws = _WeightStream([wt_hbm], wbuf, stage, sem, bn=bn, n_col_steps=n_col_steps,
                       first_col=0, first_width=bn, next_col=lambda j: j + 1, transposed=True)
    rider = _CastRider(ride_hbm, rode_hbm, ride_in, ride_out, ride_sem,
                       ws.j * pl.num_programs(1) + ws.i)
    ws.advance()
    rider.advance()
    for first, end, fn in epilogues:
        @pl.when((ws.j >= first) & (ws.j < end))
        def _(fn=fn):
            ws.cast_arrived_chunk()
            rider.cast_arrived_chunk()
            acc = lax.dot_general(a_ref[...], ws.tile(0), (((1,), (1,)), ((), ())),
                                  preferred_element_type=F32)
            o_ref[...] = fn(acc).astype(o_ref.dtype)


def _in_proj(h, w_t, n_cols, col_epilogues, ride):
    m, k = h.shape
    bm = _pick(m, (1024, 512, 256, 128, 64))
    bn = _pick(math.gcd(*[end for _, end, _ in col_epilogues]), (1024, 512, 256, 128))
    epilogues = tuple((first // bn, end // bn, fn) for first, end, fn in col_epilogues)
    n_col_steps = n_cols // bn
    n_row_steps = m // bm
    anywhere = pl.BlockSpec(memory_space=pl.ANY)
    return pl.pallas_call(
        functools.partial(_in_proj_kernel, epilogues=epilogues, n_col_steps=n_col_steps),
        grid=(n_col_steps, n_row_steps),
        in_specs=[pl.BlockSpec((bm, k), lambda j, i: (i, 0)), anywhere, anywhere],
        out_specs=[pl.BlockSpec((bm, bn), lambda j, i: (i, j)), anywhere],
        out_shape=[jax.ShapeDtypeStruct((m, n_cols), BF16),
                   jax.ShapeDtypeStruct(ride.shape, BF16)],
        scratch_shapes=(_weight_stream_scratch(1, k, bn, n_row_steps, transposed=True)
                        + _cast_rider_scratch(ride.shape, n_col_steps * n_row_steps)),
        compiler_params=_params("arbitrary", "arbitrary"),
        name="gemm_in_proj",
    )(h, w_t, ride)


def _sgu_tile(u_ref, v_ref, wm_ref, bias_ref, lng_ref, lnb_ref, o_ref, vn_ref):
    rows, width = u_ref.shape
    heads, chunk, _ = wm_ref.shape
    head_dim = width // heads
    n_chunks = rows // chunk
    v = v_ref[...].astype(F32)
    mu = jnp.mean(v, axis=-1, keepdims=True)
    vc = v - mu
    var = jnp.mean(vc * vc, axis=-1, keepdims=True)
    vn = vc * lax.rsqrt(var + EPS) * lng_ref[...] + lnb_ref[...]
    if vn_ref is not None:
        vn_ref[...] = vn
    vnb = vn.astype(BF16)
    for h in range(heads):
        lo, hi = h * head_dim, (h + 1) * head_dim
        rhs = jnp.concatenate(
            [vnb[c * chunk:(c + 1) * chunk, lo:hi] for c in range(n_chunks)], axis=1)
        mixed = _dot(wm_ref[h], rhs)
        bias = bias_ref[:, lo:hi]
        for c in range(n_chunks):
            rs = slice(c * chunk, (c + 1) * chunk)
            u = u_ref[rs, lo:hi].astype(F32)
            m = mixed[:, c * head_dim:(c + 1) * head_dim] + bias
            o_ref[rs, lo:hi] = (u * m).astype(o_ref.dtype)


def _mixers_kernel(u_ref, vg_ref, q_ref, k_ref, v_ref, r_ref, b_ref, s0_ref, gn_ref,
                   w_ref, bias_ref, lng_ref, lnb_ref, a_ref, o_ref, s_out_ref, *rest,
                   heads, dk, dv, blocks, emit_vn):
    vn_ref, s_ref, wm_ref = rest if emit_vn else (None,) + rest
    t = pl.program_id(1)

    @pl.when((pl.program_id(0) == 0) & (t == 0))
    def _():
        chunk = w_ref.shape[1]
        ri = lax.broadcasted_iota(jnp.int32, (chunk, chunk), 0)
        ci = lax.broadcasted_iota(jnp.int32, (chunk, chunk), 1)
        for h in range(w_ref.shape[0]):
            wm_ref[h] = jnp.where(ci <= ri, w_ref[h], 0.0).astype(BF16)

    @pl.when(t == 0)
    def _():
        s_ref[...] = s0_ref[0]

    _sgu_tile(u_ref, vg_ref, wm_ref, bias_ref, lng_ref, lnb_ref, a_ref, vn_ref)

    ri = lax.broadcasted_iota(jnp.int32, (GLA_BLOCK, GLA_BLOCK), 0)
    ci = lax.broadcasted_iota(jnp.int32, (GLA_BLOCK, GLA_BLOCK), 1)
    causal = ci <= ri
    half = GLA_BLOCK // 2
    for blk in range(blocks):
        rs = slice(blk * GLA_BLOCK, (blk + 1) * GLA_BLOCK)
        for h in range(heads):
            ks = slice(h * dk, (h + 1) * dk)
            vs = slice(h * dv, (h + 1) * dv)
            q = q_ref[rs, ks].astype(F32)
            k = k_ref[rs, ks].astype(F32)
            v = v_ref[rs, vs]
            b = b_ref[rs, ks]
            b_mid = b[half:half + 1, :]
            b_last = b[GLA_BLOCK - 1:GLA_BLOCK, :]
            q_in = (q * jnp.exp(b - b_mid)).astype(BF16)
            k_in = (k * jnp.exp(b_mid - b)).astype(BF16)
            att = lax.dot_general(q_in, k_in, (((1,), (1,)), ((), ())),
                                  preferred_element_type=F32)
            att = jnp.where(causal, att, 0.0).astype(BF16)
            s = s_ref[h]
            o = _dot(att, v) + _dot((q * jnp.exp(b)).astype(BF16), s.astype(BF16))
            k_out = (k * jnp.exp(b_last - b)).astype(BF16)
            kv = lax.dot_general(k_out, v, (((0,), (0,)), ((), ())),
                                 preferred_element_type=F32)
            decay_col = jnp.transpose(
                jnp.broadcast_to(jnp.exp(b_last), (LANES, dk)))
            decay = jnp.concatenate([decay_col] * (dv // LANES), axis=1)
            s_ref[h] = decay * s + kv
            o_n = _rms(o, gn_ref[...])
            o_ref[rs, vs] = (o_n * r_ref[rs, vs].astype(F32)).astype(o_ref.dtype)

    @pl.when(t == pl.num_programs(1) - 1)
    def _():
        s_out_ref[0] = s_ref[...]


def _mixers(proj, b_all, s0, gn, w_s, bias_full, ln_g, ln_b, row0, n_streams, stream_len, *,
            col_q, col_k, col_v, col_r, sgu_chunk, emit_vn):
    _, heads, dk, dv = s0.shape
    kd, vd = heads * dk, heads * dv
    a_width = ln_g.shape[1]
    blocks = _pick(stream_len // GLA_BLOCK, (4, 2, 1))
    rows = blocks * GLA_BLOCK
    assert rows % sgu_chunk == 0, "an SGU chunk may not straddle two row tiles"
    steps = stream_len // rows
    r0 = row0 // rows
    cq, ck, cv, cr = col_q // kd, col_k // kd, col_v // vd, col_r // vd
    assert w_s.shape[1:] == (sgu_chunk, sgu_chunk)
    kern = functools.partial(_mixers_kernel, heads=heads, dk=dk, dv=dv, blocks=blocks,
                             emit_vn=emit_vn)
    row = lambda s, t: r0 + s * steps + t
    out_row = lambda s, t: s * steps + t
    n_rows = n_streams * stream_len
    const = lambda shape: pl.BlockSpec(shape, lambda s, t: (0,) * len(shape))
    out_specs = [
        pl.BlockSpec((rows, a_width), lambda s, t: (out_row(s, t), 0)),
        pl.BlockSpec((rows, vd), lambda s, t: (out_row(s, t), 0)),
        pl.BlockSpec((1, heads, dk, dv), lambda s, t: (s, 0, 0, 0)),
    ]
    out_shape = [
        jax.ShapeDtypeStruct((n_rows, a_width), BF16),
        jax.ShapeDtypeStruct((n_rows, vd), BF16),
        jax.ShapeDtypeStruct(s0.shape, F32),
    ]
    if emit_vn:
        out_specs.append(pl.BlockSpec((rows, a_width), lambda s, t: (out_row(s, t), 0)))
        out_shape.append(jax.ShapeDtypeStruct((n_rows, a_width), F32))
    return pl.pallas_call(
        kern,
        grid=(n_streams, steps),
        in_specs=[
            pl.BlockSpec((rows, a_width), lambda s, t: (row(s, t), 0)),
            pl.BlockSpec((rows, a_width), lambda s, t: (row(s, t), 1)),
            pl.BlockSpec((rows, kd), lambda s, t: (row(s, t), cq)),
            pl.BlockSpec((rows, kd), lambda s, t: (row(s, t), ck)),
            pl.BlockSpec((rows, vd), lambda s, t: (row(s, t), cv)),
            pl.BlockSpec((rows, vd), lambda s, t: (row(s, t), cr)),
            pl.BlockSpec((rows, kd), lambda s, t: (row(s, t), 0)),
            pl.BlockSpec((1, heads, dk, dv), lambda s, t: (s, 0, 0, 0)),
            const((1, dv)), const(w_s.shape), const(bias_full.shape),
            const((1, a_width)), const((1, a_width)),
        ],
        out_specs=out_specs,
        out_shape=out_shape,
        scratch_shapes=[pltpu.VMEM((heads, dk, dv), F32), pltpu.VMEM(w_s.shape, BF16)],
        compiler_params=_params("arbitrary", "arbitrary"),
        name="sgu_gla_mixers",
    )(proj, proj, proj, proj, proj, proj, b_all, s0, gn, w_s, bias_full, ln_g, ln_b)


def _mix_kernel(ap_ref, as_ref, bp_ref, bs_ref, xp_ref, xs_ref, wa_ref, wb_ref, g_ref,
                x1_ref, xg_ref, ssq_ref, *, p_tiles):
    j = pl.program_id(1)

    def body(a_ref, b_ref, x_ref):
        half = x_ref.shape[0] // 2
        for rs in (slice(0, half), slice(half, 2 * half)):
            x1 = x_ref[rs, :] + (_dot(a_ref[rs, :], wa_ref[...]) + _dot(b_ref[rs, :], wb_ref[...]))
            x1_ref[rs, :] = x1
            xg_ref[rs, :] = (x1 * g_ref[...]).astype(xg_ref.dtype)
            sq = x1 * x1
            part = sq[:, :LANES]
            for c in range(1, sq.shape[1] // LANES):
                part = part + sq[:, c * LANES:(c + 1) * LANES]
            ssq_ref[rs, :] += part

    @pl.when(j == 0)
    def _():
        ssq_ref[...] = jnp.zeros_like(ssq_ref)

    _on_prompt_or_sample(pl.program_id(0), p_tiles, body,
                         (ap_ref, bp_ref, xp_ref), (as_ref, bs_ref, xs_ref))


def _mix(a_p, a_s, b_p, b_s, x_p, x_s, w_out, g):
    (n_p, ka), n_s = a_p.shape, a_s.shape[0]
    m = n_p + n_s
    kb = b_p.shape[1]
    n = w_out.shape[1]
    assert ka == kb, "head groups of different widths need separate weight specs"
    bm = _pick(math.gcd(n_p, n_s), (1024, 512, 256, 128, 64))
    bn = _pick(n, (512, 256, 128))
    p_tiles = n_p // bm
    split = functools.partial(_split_specs, p_tiles=p_tiles, single_buffer_sample=True)
    return pl.pallas_call(
        functools.partial(_mix_kernel, p_tiles=p_tiles),
        grid=(m // bm, n // bn),
        in_specs=(split((bm, ka)) + split((bm, kb)) + split((bm, bn), col_tiles=n // bn) + [
            pl.BlockSpec((ka, bn), lambda i, j: (0, j)),
            pl.BlockSpec((kb, bn), lambda i, j: (1, j)),
            pl.BlockSpec((1, bn), lambda i, j: (0, j)),
        ]),
        out_specs=[
            pl.BlockSpec((bm, bn), lambda i, j: (i, j)),
            pl.BlockSpec((bm, bn), lambda i, j: (i, j)),
            pl.BlockSpec((bm, LANES), lambda i, j: (i, 0)),
        ],
        out_shape=[
            jax.ShapeDtypeStruct((m, n), F32),
            jax.ShapeDtypeStruct((m, n), BF16),
            jax.ShapeDtypeStruct((m, LANES), F32),
        ],
        compiler_params=_params("parallel", "arbitrary"),
        name="gemm_out_proj_residual",
    )(a_p, a_s, b_p, b_s, x_p, x_s, w_out, w_out, g)


def _ffn_up_kernel(xg_ref, ssq_ref, wg_hbm, wu_hbm, ride_hbm, o_ref, rode_hbm, wbuf, stage, sem,
                   ride_in, ride_out, ride_sem, *, d_model, d_ff):
    bn = o_ref.shape[1]
    n_steps = -(-d_ff // bn)
    edge = d_ff - (n_steps - 1) * bn
    edge_first = edge != bn
    ws = _WeightStream([wg_hbm, wu_hbm], wbuf, stage, sem, bn=bn, n_col_steps=n_steps,
                       first_col=n_steps - 1 if edge_first else 0, first_width=edge,
                       next_col=(lambda j: j) if edge_first else (lambda j: j + 1))
    rider = _CastRider(ride_hbm, rode_hbm, ride_in, ride_out, ride_sem,
                       ws.j * pl.num_programs(1) + ws.i)
    ws.advance()
    rider.advance()

    def body(width):
        ws.cast_arrived_chunk()
        rider.cast_arrived_chunk()
        rinv = lax.rsqrt(jnp.sum(ssq_ref[...], axis=-1, keepdims=True) * (1.0 / d_model) + EPS)
        h = xg_ref[...]
        g = _dot(h, ws.tile(0, width)) * rinv
        u = _dot(h, ws.tile(1, width)) * rinv
        o_ref[:, :width] = (_silu(g) * u).astype(o_ref.dtype)

    if edge_first:
        pl.when(ws.j == 0)(lambda: body(edge))
        pl.when(ws.j > 0)(lambda: body(bn))
    else:
        body(bn)


def _ffn_up(xg, ssq, wg, wu, ride):
    m, k = xg.shape
    f = wg.shape[1]
    bm = _pick(m, (1024, 512, 256, 128, 64))
    bn = 512
    assert (f % bn) % V7X_MXU_DIM == 0, "the edge column block must stay MXU-column aligned"
    n_steps = pl.cdiv(f, bn)
    n_row_steps = m // bm
    col = (lambda j: j) if f % bn == 0 else (lambda j: jnp.where(j == 0, n_steps - 1, j - 1))
    anywhere = pl.BlockSpec(memory_space=pl.ANY)
    return pl.pallas_call(
        functools.partial(_ffn_up_kernel, d_model=k, d_ff=f),
        grid=(n_steps, n_row_steps),
        in_specs=[
            pl.BlockSpec((bm, k), lambda j, i: (i, 0)),
            pl.BlockSpec((bm, LANES), lambda j, i: (i, 0)),
            anywhere, anywhere, anywhere,
        ],
        out_specs=[pl.BlockSpec((bm, bn), lambda j, i: (i, col(j))), anywhere],
        out_shape=[jax.ShapeDtypeStruct((m, f), BF16), jax.ShapeDtypeStruct(ride.shape, BF16)],
        scratch_shapes=(_weight_stream_scratch(2, k, bn, n_row_steps)
                        + _cast_rider_scratch(ride.shape, n_steps * n_row_steps)),
        compiler_params=_params("arbitrary", "arbitrary"),
        name="ffn_gate_up",
    )(xg, ssq, wg, wu, ride)


def _ffn_down_kernel(h_ref, w_ref, x_ref, o_ref, *, d_ff):
    k = pl.program_id(2)
    bk = h_ref.shape[1]
    n_steps = -(-d_ff // bk)
    last = d_ff - (n_steps - 1) * bk

    def body(width, first):
        p = _dot(h_ref[:, :width], w_ref[:width, :])
        if first:
            o_ref[...] = x_ref[...] + p
        else:
            o_ref[...] += p

    if n_steps == 1:
        body(last, True)
        return
    pl.when(k == 0)(lambda: body(bk, True))
    if n_steps > 2:
        pl.when((k > 0) & (k < n_steps - 1))(lambda: body(bk, False))
    pl.when(k == n_steps - 1)(lambda: body(last, False))


def _ffn_down(h1, wd, x1):
    m, f = h1.shape
    n = wd.shape[1]
    bm = _pick(m, (1024, 512, 256, 128, 64))
    bn = _pick(n, (1024, 512, 256, 128))
    bk = 2816
    assert (f % bk) % V7X_MXU_DIM == 0, "the last contraction block must stay MXU-row aligned"
    return pl.pallas_call(
        functools.partial(_ffn_down_kernel, d_ff=f),
        grid=(m // bm, n // bn, pl.cdiv(f, bk)),
        in_specs=[
            pl.BlockSpec((bm, bk), lambda i, j, k: (i, k)),
            pl.BlockSpec((bk, bn), lambda i, j, k: (k, j)),
            pl.BlockSpec((bm, bn), lambda i, j, k: (i, j)),
        ],
        out_specs=pl.BlockSpec((bm, bn), lambda i, j, k: (i, j)),
        out_shape=jax.ShapeDtypeStruct((m, n), F32),
        compiler_params=_params("parallel", "parallel", "arbitrary"),
        name="ffn_down_residual",
    )(h1, wd, x1)


def _ffn_down_norm_kernel(h_ref, w_ref, x_ref, g_ref, y_ref, *, d_ff):
    k = pl.program_id(1)
    bk = h_ref.shape[1]
    n_steps = -(-d_ff // bk)
    last = d_ff - (n_steps - 1) * bk

    def body(width, first, final):
        p = _dot(h_ref[:, :width], w_ref[:width, :])
        if first:
            y_ref[...] = x_ref[...] + p
        else:
            y_ref[...] += p
        if final:
            y_ref[...] = _rms(y_ref[...], g_ref[...])

    if n_steps == 1:
        body(last, True, True)
        return
    pl.when(k == 0)(lambda: body(bk, True, False))
    if n_steps > 2:
        pl.when((k > 0) & (k < n_steps - 1))(lambda: body(bk, False, False))
    pl.when(k == n_steps - 1)(lambda: body(last, False, True))


def _ffn_down_norm(h1, wd, x1, g, row0, n_rows):
    f = h1.shape[1]
    n = wd.shape[1]
    bm = _pick(math.gcd(n_rows, row0) if row0 else n_rows, (512, 256, 128, 64))
    bk = 1024
    assert (f % bk) % V7X_MXU_DIM == 0, "the last contraction block must stay MXU-row aligned"
    r0 = row0 // bm
    return pl.pallas_call(
        functools.partial(_ffn_down_norm_kernel, d_ff=f),
        grid=(n_rows // bm, pl.cdiv(f, bk)),
        in_specs=[
            pl.BlockSpec((bm, bk), lambda i, k: (r0 + i, k)),
            pl.BlockSpec((bk, n), lambda i, k: (k, 0)),
            pl.BlockSpec((bm, n), lambda i, k: (r0 + i, 0)),
            pl.BlockSpec((1, n), lambda i, k: (0, 0)),
        ],
        out_specs=pl.BlockSpec((bm, n), lambda i, k: (i, 0)),
        out_shape=jax.ShapeDtypeStruct((n_rows, n), F32),
        compiler_params=_params("parallel", "arbitrary"),
        name="ffn_down_final_norm",
    )(h1, wd, x1, g)


def kernel(x_prompt, x_sample, state_gla, g_mix, w_in, w_s, b_s, ln_g, ln_b, w_gate_up, b_gate,
           gla_norm_g, w_out, g_ffn, w_ffn_gate, w_ffn_up, w_ffn_down, g_final):
    depth = g_mix.shape[0]
    n_p, len_p, d_model = x_prompt.shape
    n_s, len_s, _ = x_sample.shape
    rows_p, rows_s = n_p * len_p, n_s * len_s
    _, _, heads_b, dk, dv = state_gla.shape
    a_width = ln_g.shape[-1]
    a_heads, sgu_chunk = w_s.shape[1], w_s.shape[2]
    key_dim, val_dim = heads_b * dk, heads_b * dv
    rank = w_gate_up.shape[1]
    d_ff = w_ffn_gate.shape[-1]
    main_cols = 2 * a_width + 2 * key_dim + 2 * val_dim
    col_q = 2 * a_width
    col_k = col_q + key_dim
    col_v = col_k + key_dim
    col_r = col_v + val_dim
    chunk_p = min(sgu_chunk, len_p)
    chunk_s = min(sgu_chunk, len_s)

    x_p = x_prompt.reshape(rows_p, d_model)
    x_s = x_sample.reshape(rows_s, d_model)
    zero_state = jnp.zeros((n_p, heads_b, dk, dv), state_gla.dtype)

    gla_p, gla_s, vn_s = [], [], []
    for d in range(depth):
        w_up = jnp.pad(w_gate_up[d], ((0, LANES - rank), (0, 0))).astype(BF16)
        head_dim = a_width // a_heads
        bias_full = jnp.repeat(b_s[d].T, head_dim, axis=1)
        row = lambda v: v.reshape(1, -1)

        w_in_t = w_in[d].T
        h, b_all = _prep(x_p, x_s, row(g_mix[d]), w_in_t, main_cols, rank, w_up, row(b_gate[d]))
        q_scale = dk ** -0.5
        proj, w_o = _in_proj(h, w_in_t, main_cols, (
            (0, col_q, _gelu_tanh),
            (col_q, col_k, lambda t: t * q_scale),
            (col_k, col_r, lambda t: t),
            (col_r, main_cols, _silu),
        ), ride=w_out[d])

        cols = dict(col_q=col_q, col_k=col_k, col_v=col_v, col_r=col_r)
        shared = (row(gla_norm_g[d]),)
        ln = (row(ln_g[d]), row(ln_b[d]))
        a_p, o_p, s_p = _mixers(
            proj, b_all, zero_state, *shared, w_s[d][:, :chunk_p, :chunk_p], bias_full[:chunk_p],
            *ln, 0, n_p, len_p, **cols, sgu_chunk=chunk_p, emit_vn=False)
        a_s, o_s, s_s, vn = _mixers(
            proj, b_all, state_gla[d], *shared, w_s[d][:, :chunk_s, :chunk_s], bias_full[:chunk_s],
            *ln, rows_p, n_s, len_s, **cols, sgu_chunk=chunk_s, emit_vn=True)

        x1, xg, ssq = _mix(a_p, a_s, o_p, o_s, x_p, x_s, w_o, row(g_ffn[d]))
        h1, wd = _ffn_up(xg, ssq, w_ffn_gate[d], w_ffn_up[d], ride=w_ffn_down[d])
        if d + 1 < depth:
            x = _ffn_down(h1, wd, x1)
            x_p, x_s = x[:rows_p], x[rows_p:]
        else:
            gf = g_final.reshape(1, -1)
            y_p = _ffn_down_norm(h1, wd, x1, gf, 0, rows_p).reshape(n_p, len_p, d_model)
            y_s = _ffn_down_norm(h1, wd, x1, gf, rows_p, rows_s).reshape(n_s, len_s, d_model)

        gla_p.append(s_p)
        gla_s.append(s_s)
        vn_s.append(vn.reshape(n_s, len_s, a_width))

    return (y_p, y_s, jnp.stack(gla_p), jnp.stack(gla_s), jnp.stack(vn_s))
```

```python
import functools
import math

import jax
import jax.numpy as jnp
from jax import lax
from jax.experimental import pallas as pl
from jax.experimental.pallas import tpu as pltpu

F32 = jnp.float32
BF16 = jnp.bfloat16

EPS = 1e-6
GLA_BLOCK = 64
GATE_TAU = 16.0
LANES = 128
BF16_SUBLANES = 16
V7X_MXU_DIM = 256
V7X_VMEM_LIMIT_BYTES = 56 * 1024 * 1024


def _pick(n, candidates):
    for c in candidates:
        if n % c == 0:
            return c
    raise ValueError(f"no tile in {candidates} divides {n}")


def _params(*semantics):
    return pltpu.CompilerParams(dimension_semantics=semantics,
                                vmem_limit_bytes=V7X_VMEM_LIMIT_BYTES)


def _dot(a, b):
    return jnp.dot(a, b, preferred_element_type=F32)


def _gelu_tanh(x):
    c = math.sqrt(2.0 / math.pi)
    return 0.5 * x * (1.0 + jnp.tanh(c * (x + 0.044715 * (x * x * x))))


def _silu(x):
    return x * (1.0 / (1.0 + jnp.exp(-x)))


def _rms(x, g):
    return x * lax.rsqrt(jnp.mean(x * x, axis=-1, keepdims=True) + EPS) * g


def _prep_kernel(xp_ref, xs_ref, g_ref, win_ref, wup_ref, bg_ref, tri_ref, h_ref, b_ref, wlr_ref,
                 *, p_tiles, sub, rank):
    @pl.when(pl.program_id(0) == 0)
    def _():
        pad = jnp.zeros((LANES - rank, win_ref.shape[1]), F32)
        wlr_ref[...] = jnp.transpose(jnp.concatenate([win_ref[...], pad], axis=0)).astype(BF16)

    def body(x_ref):
        for c in range(x_ref.shape[0] // sub):
            rs = slice(c * sub, (c + 1) * sub)
            hb = _rms(x_ref[rs, :], g_ref[...]).astype(BF16)
            h_ref[rs, :] = hb
            g_lr = _dot(hb, wlr_ref[...])
            z = _dot(g_lr.astype(BF16), wup_ref[...]) + bg_ref[...]
            log_a = (jnp.minimum(z, 0.0) - jnp.log1p(jnp.exp(-jnp.abs(z)))) * (1.0 / GATE_TAU)
            hi = log_a.astype(BF16)
            r1 = log_a - hi.astype(F32)
            mid = r1.astype(BF16)
            lo = (r1 - mid.astype(F32)).astype(BF16)
            tri = tri_ref[...]
            b_ref[rs, :] = _dot(tri, hi) + _dot(tri, mid) + _dot(tri, lo)

    _on_prompt_or_sample(pl.program_id(0), p_tiles, body, (xp_ref,), (xs_ref,))


def _on_prompt_or_sample(i, p_tiles, body, prompt_refs, sample_refs):
    @pl.when(i < p_tiles)
    def _():
        body(*prompt_refs)

    @pl.when(i >= p_tiles)
    def _():
        body(*sample_refs)


def _split_specs(block, p_tiles, col_tiles=None, single_buffer_sample=False):
    def prompt_map(i, *g):
        col = jnp.where(i < p_tiles, g[0], col_tiles - 1) if col_tiles else 0
        return jnp.minimum(i, p_tiles - 1), col

    def sample_map(i, *g):
        col = jnp.where(i >= p_tiles, g[0], 0) if col_tiles else 0
        return jnp.maximum(i - p_tiles, 0), col

    mode = dict(pipeline_mode=pl.Buffered(1)) if single_buffer_sample else {}
    return [pl.BlockSpec(block, prompt_map), pl.BlockSpec(block, sample_map, **mode)]


def _prep(x_p, x_s, g, w_in_t, gate_row0, rank, w_up, b_gate):
    (n_p, d), n_s = x_p.shape, x_s.shape[0]
    n = n_p + n_s
    key_dim = w_up.shape[1]
    assert gate_row0 % rank == 0 and rank % 8 == 0 and rank <= LANES
    bm = _pick(math.gcd(n_p, n_s), (512, 256, 128, 64))
    sub = _pick(bm, (128, 64))
    idx = jnp.arange(sub)
    tri = ((idx[None, :] <= idx[:, None])
           & (idx[None, :] // GLA_BLOCK == idx[:, None] // GLA_BLOCK)).astype(BF16)
    p_tiles = n_p // bm
    const = lambda shape: pl.BlockSpec(shape, lambda i: (0, 0))
    return pl.pallas_call(
        functools.partial(_prep_kernel, p_tiles=p_tiles, sub=sub, rank=rank),
        grid=(n // bm,),
        in_specs=_split_specs((bm, d), p_tiles, single_buffer_sample=True) + [
            const((1, d)),
            pl.BlockSpec((rank, d), lambda i: (gate_row0 // rank, 0)),
            const(w_up.shape), const((1, key_dim)), const((sub, sub)),
        ],
        out_specs=[
            pl.BlockSpec((bm, d), lambda i: (i, 0)),
            pl.BlockSpec((bm, key_dim), lambda i: (i, 0)),
        ],
        out_shape=[
            jax.ShapeDtypeStruct((n, d), BF16),
            jax.ShapeDtypeStruct((n, key_dim), F32),
        ],
        scratch_shapes=[pltpu.VMEM((d, LANES), BF16)],
        compiler_params=_params("arbitrary"),
        name="prep_rmsnorm_gate",
    )(x_p, x_s, g, w_in_t, w_up, b_gate, tri)


class _WeightStream:
    def __init__(self, w_hbms, wbuf, stage, sem, *, bn, n_col_steps, first_col, first_width,
                 next_col, transposed=False):
        self.w_hbms, self.wbuf, self.stage, self.sem = w_hbms, wbuf, stage, sem
        self.transposed = transposed
        self.bn = bn
        self.rows = stage.shape[2]
        self.tile_rows = wbuf.shape[2] - self.rows
        self.n_chunks = self.tile_rows // self.rows
        self.n_col_steps = n_col_steps
        self.first_col, self.first_width = first_col, first_width
        self.next_col = next_col
        self.j = pl.program_id(0)
        self.i = pl.program_id(1)

    def _copy(self, m, chunk, col0, width, slot):
        if self.transposed:
            assert width == self.bn
            src = self.w_hbms[m].at[pl.ds(col0 + chunk * self.rows, self.rows), :]
            dst = self.stage.at[m, slot]
        else:
            src = self.w_hbms[m].at[pl.ds(chunk * self.rows, self.rows), pl.ds(col0, width)]
            dst = self.stage.at[m, slot, :, pl.ds(0, width)]
        return pltpu.make_async_copy(src, dst, self.sem.at[m, slot])

    def _next_copy(self, m, chunk):
        col0 = pl.multiple_of(self.next_col(self.j) * self.bn, self.bn)
        return self._copy(m, chunk, col0, self.bn, chunk % 2)

    def advance(self):
        j, i, n_mats = self.j, self.i, len(self.w_hbms)

        @pl.when((j == 0) & (i == 0))
        def _():
            self.stage[...] = jnp.zeros_like(self.stage)
            for c in range(self.n_chunks):
                for m in range(n_mats):
                    cp = self._copy(m, c, self.first_col * self.bn, self.first_width, c % 2)
                    cp.start()
                    cp.wait()
                    self._store_chunk(m, 0, c, c % 2)

        has_next = j + 1 < self.n_col_steps

        @pl.when(has_next & (i >= 1) & (i <= self.n_chunks))
        def _():
            for m in range(n_mats):
                self._next_copy(m, i - 1).wait()

        @pl.when(has_next & (i < self.n_chunks))
        def _():
            for m in range(n_mats):
                self._next_copy(m, i).start()

    def _store_chunk(self, m, tile_slot, chunk, stage_slot):
        row0 = chunk * self.rows
        if not isinstance(row0, int):
            row0 = pl.multiple_of(row0, self.rows)
        self.wbuf[m, tile_slot, pl.ds(row0, self.rows), :] = self.stage[m, stage_slot].astype(BF16)

    def cast_arrived_chunk(self):
        j, i = self.j, self.i
        valid = (j + 1 < self.n_col_steps) & (i >= 1) & (i <= self.n_chunks)
        chunk = jnp.where(valid, i - 1, self.n_chunks)
        for m in range(len(self.w_hbms)):
            self._store_chunk(m, (j + 1) % 2, chunk, (i + 1) % 2)

    def tile(self, m, width=None):
        w = self.wbuf[m, self.j % 2, :self.tile_rows, :]
        return w if width is None else w[:, :width]


class _CastRider:
    def __init__(self, src_hbm, dst_hbm, in_buf, out_buf, sem, step):
        self.src, self.dst, self.in_buf, self.out_buf, self.sem = src_hbm, dst_hbm, in_buf, out_buf, sem
        self.rows = in_buf.shape[1]
        self.n_chunks = src_hbm.shape[0] // self.rows
        self.step = step

    def _fetch(self, chunk):
        rs = pl.ds(pl.multiple_of(chunk * self.rows, self.rows), self.rows)
        return pltpu.make_async_copy(self.src.at[rs, :], self.in_buf.at[chunk % 2],
                                     self.sem.at[0, chunk % 2])

    def _write_back(self, chunk):
        rs = pl.ds(pl.multiple_of(chunk * self.rows, self.rows), self.rows)
        return pltpu.make_async_copy(self.out_buf.at[chunk % 2], self.dst.at[rs, :],
                                     self.sem.at[1, chunk % 2])

    def advance(self):
        g, n = self.step, self.n_chunks

        @pl.when(g == 0)
        def _():
            self.in_buf[...] = jnp.zeros_like(self.in_buf)

        steady = (g >= 3) & (g < n)

        @pl.when(steady)
        def _():
            self._write_back(g - 3).wait()
            self._fetch(g - 1).wait()
            self._fetch(g).start()
            self._write_back(g - 2).start()

        @pl.when(jnp.logical_not(steady))
        def _():
            pl.when((g >= 3) & (g < n + 3))(lambda: self._write_back(g - 3).wait())
            pl.when((g >= 1) & (g < n + 1))(lambda: self._fetch(g - 1).wait())
            pl.when(g < n)(lambda: self._fetch(g).start())
            pl.when((g >= 2) & (g < n + 2))(lambda: self._write_back(g - 2).start())

    def cast_arrived_chunk(self):
        slot = (self.step + 1) % 2
        self.out_buf[slot] = self.in_buf[slot].astype(BF16)


def _cast_rider_scratch(src_shape, n_steps):
    n_rows, n_cols = src_shape
    rows = BF16_SUBLANES
    while n_rows % rows or n_rows // rows + 3 > n_steps:
        rows += BF16_SUBLANES
        assert rows <= n_rows, "too few grid steps to ride the cast on"
    return [
        pltpu.VMEM((2, rows, n_cols), F32),
        pltpu.VMEM((2, rows, n_cols), BF16),
        pltpu.SemaphoreType.DMA((2, 2)),
    ]


_MAX_WEIGHT_CHUNKS = 16


def _weight_stream_scratch(n_mats, k, bn, n_row_steps, transposed=False):
    assert n_row_steps >= 2, "weight chunks are prefetched across the row steps of one column tile"
    tile_rows, tile_cols = (bn, k) if transposed else (k, bn)
    n_chunks = 1
    while (n_chunks * 2 <= min(n_row_steps - 1, _MAX_WEIGHT_CHUNKS)
           and tile_rows % (n_chunks * 2 * BF16_SUBLANES) == 0):
        n_chunks *= 2
    rows = tile_rows // n_chunks
    return [
        pltpu.VMEM((n_mats, 2, tile_rows + rows, tile_cols), BF16),
        pltpu.VMEM((n_mats, 2, rows, tile_cols), F32),
        pltpu.SemaphoreType.DMA((n_mats, 2)),
    ]


def _in_proj_kernel(a_ref, wt_hbm, ride_hbm, o_ref, rode_hbm, wbuf, stage, sem,
                    ride_in, ride_out, ride_sem, *, epilogues, n_col_steps):
    bn = o_ref.shape[1]
    ws = _WeightStream([wt_hbm], wbuf, stage, sem, bn=bn, n_col_steps=n_col_steps,
                       first_col=0, first_width=bn, next_col=lambda j: j + 1, transposed=True)
    rider = _CastRider(ride_hbm, rode_hbm, ride_in, ride_out, ride_sem,
                       ws.j * pl.num_programs(1) + ws.i)
    ws.advance()
    rider.advance()
    for first, end, fn in epilogues:
        @pl.when((ws.j >= first) & (ws.j < end))
        def _(fn=fn):
            ws.cast_arrived_chunk()
            rider.cast_arrived_chunk()
            acc = lax.dot_general(a_ref[...], ws.tile(0), (((1,), (1,)), ((), ())),
                                  preferred_element_type=F32)
            o_ref[...] = fn(acc).astype(o_ref.dtype)


def _in_proj(h, w_t, n_cols, col_epilogues, ride):
    m, k = h.shape
    bm = _pick(m, (1024, 512, 256, 128, 64))
    bn = _pick(math.gcd(*[end for _, end, _ in col_epilogues]), (1024, 512, 256, 128))
    epilogues = tuple((first // bn, end // bn, fn) for first, end, fn in col_epilogues)
    n_col_steps = n_cols // bn
    n_row_steps = m // bm
    anywhere = pl.BlockSpec(memory_space=pl.ANY)
    return pl.pallas_call(
        functools.partial(_in_proj_kernel, epilogues=epilogues, n_col_steps=n_col_steps),
        grid=(n_col_steps, n_row_steps),
        in_specs=[pl.BlockSpec((bm, k), lambda j, i: (i, 0)), anywhere, anywhere],
        out_specs=[pl.BlockSpec((bm, bn), lambda j, i: (i, j)), anywhere],
        out_shape=[jax.ShapeDtypeStruct((m, n_cols), BF16),
                   jax.ShapeDtypeStruct(ride.shape, BF16)],
        scratch_shapes=(_weight_stream_scratch(1, k, bn, n_row_steps, transposed=True)
                        + _cast_rider_scratch(ride.shape, n_col_steps * n_row_steps)),
        compiler_params=_params("arbitrary", "arbitrary"),
        name="gemm_in_proj",
    )(h, w_t, ride)


def _sgu_tile(u_ref, v_ref, wm_ref, bias_ref, lng_ref, lnb_ref, o_ref, vn_ref):
    rows, width = u_ref.shape
    heads, chunk, _ = wm_ref.shape
    head_dim = width // heads
    n_chunks = rows // chunk
    v = v_ref[...].astype(F32)
    mu = jnp.mean(v, axis=-1, keepdims=True)
    vc = v - mu
    var = jnp.mean(vc * vc, axis=-1, keepdims=True)
    vn = vc * lax.rsqrt(var + EPS) * lng_ref[...] + lnb_ref[...]
    if vn_ref is not None:
        vn_ref[...] = vn
    vnb = vn.astype(BF16)
    for h in range(heads):
        lo, hi = h * head_dim, (h + 1) * head_dim
        rhs = jnp.concatenate(
            [vnb[c * chunk:(c + 1) * chunk, lo:hi] for c in range(n_chunks)], axis=1)
        mixed = _dot(wm_ref[h], rhs)
        bias = bias_ref[:, lo:hi]
        for c in range(n_chunks):
            rs = slice(c * chunk, (c + 1) * chunk)
            u = u_ref[rs, lo:hi].astype(F32)
            m = mixed[:, c * head_dim:(c + 1) * head_dim] + bias
            o_ref[rs, lo:hi] = (u * m).astype(o_ref.dtype)


def _mixers_kernel(u_ref, vg_ref, q_ref, k_ref, v_ref, r_ref, b_ref, s0_ref, gn_ref,
                   w_ref, bias_ref, lng_ref, lnb_ref, a_ref, o_ref, s_out_ref, *rest,
                   heads, dk, dv, blocks, emit_vn):
    vn_ref, s_ref, wm_ref = rest if emit_vn else (None,) + rest
    t = pl.program_id(1)

    @pl.when((pl.program_id(0) == 0) & (t == 0))
    def _():
        chunk = w_ref.shape[1]
        ri = lax.broadcasted_iota(jnp.int32, (chunk, chunk), 0)
        ci = lax.broadcasted_iota(jnp.int32, (chunk, chunk), 1)
        for h in range(w_ref.shape[0]):
            wm_ref[h] = jnp.where(ci <= ri, w_ref[h], 0.0).astype(BF16)

    @pl.when(t == 0)
    def _():
        s_ref[...] = s0_ref[0]

    _sgu_tile(u_ref, vg_ref, wm_ref, bias_ref, lng_ref, lnb_ref, a_ref, vn_ref)

    ri = lax.broadcasted_iota(jnp.int32, (GLA_BLOCK, GLA_BLOCK), 0)
    ci = lax.broadcasted_iota(jnp.int32, (GLA_BLOCK, GLA_BLOCK), 1)
    causal = ci <= ri
    half = GLA_BLOCK // 2
    for blk in range(blocks):
        rs = slice(blk * GLA_BLOCK, (blk + 1) * GLA_BLOCK)
        for h in range(heads):
            ks = slice(h * dk, (h + 1) * dk)
            vs = slice(h * dv, (h + 1) * dv)
            q = q_ref[rs, ks].astype(F32)
            k = k_ref[rs, ks].astype(F32)
            v = v_ref[rs, vs]
            b = b_ref[rs, ks]
            b_mid = b[half:half + 1, :]
            b_last = b[GLA_BLOCK - 1:GLA_BLOCK, :]
            q_in = (q * jnp.exp(b - b_mid)).astype(BF16)
            k_in = (k * jnp.exp(b_mid - b)).astype(BF16)
            att = lax.dot_general(q_in, k_in, (((1,), (1,)), ((), ())),
                                  preferred_element_type=F32)
            att = jnp.where(causal, att, 0.0).astype(BF16)
            s = s_ref[h]
            o = _dot(att, v) + _dot((q * jnp.exp(b)).astype(BF16), s.astype(BF16))
            k_out = (k * jnp.exp(b_last - b)).astype(BF16)
            kv = lax.dot_general(k_out, v, (((0,), (0,)), ((), ())),
                                 preferred_element_type=F32)
            decay_col = jnp.transpose(
                jnp.broadcast_to(jnp.exp(b_last), (LANES, dk)))
            decay = jnp.concatenate([decay_col] * (dv // LANES), axis=1)
            s_ref[h] = decay * s + kv
            o_n = _rms(o, gn_ref[...])
            o_ref[rs, vs] = (o_n * r_ref[rs, vs].astype(F32)).astype(o_ref.dtype)

    @pl.when(t == pl.num_programs(1) - 1)
    def _():
        s_out_ref[0] = s_ref[...]


def _mixers(proj, b_all, s0, gn, w_s, bias_full, ln_g, ln_b, row0, n_streams, stream_len, *,
            col_q, col_k, col_v, col_r, sgu_chunk, emit_vn):
    _, heads, dk, dv = s0.shape
    kd, vd = heads * dk, heads * dv
    a_width = ln_g.shape[1]
    blocks = _pick(stream_len // GLA_BLOCK, (8, 4, 2, 1))
    rows = blocks * GLA_BLOCK
    assert rows % sgu_chunk == 0, "an SGU chunk may not straddle two row tiles"
    steps = stream_len // rows
    r0 = row0 // rows
    cq, ck, cv, cr = col_q // kd, col_k // kd, col_v // vd, col_r // vd
    assert w_s.shape[1:] == (sgu_chunk, sgu_chunk)
    kern = functools.partial(_mixers_kernel, heads=heads, dk=dk, dv=dv, blocks=blocks,
                             emit_vn=emit_vn)
    row = lambda s, t: r0 + s * steps + t
    out_row = lambda s, t: s * steps + t
    n_rows = n_streams * stream_len
    const = lambda shape: pl.BlockSpec(shape, lambda s, t: (0,) * len(shape))
    out_specs = [
        pl.BlockSpec((rows, a_width), lambda s, t: (out_row(s, t), 0)),
        pl.BlockSpec((rows, vd), lambda s, t: (out_row(s, t), 0)),
        pl.BlockSpec((1, heads, dk, dv), lambda s, t: (s, 0, 0, 0)),
    ]
    out_shape = [
        jax.ShapeDtypeStruct((n_rows, a_width), BF16),
        jax.ShapeDtypeStruct((n_rows, vd), BF16),
        jax.ShapeDtypeStruct(s0.shape, F32),
    ]
    if emit_vn:
        out_specs.append(pl.BlockSpec((rows, a_width), lambda s, t: (out_row(s, t), 0)))
        out_shape.append(jax.ShapeDtypeStruct((n_rows, a_width), F32))
    return pl.pallas_call(
        kern,
        grid=(n_streams, steps),
        in_specs=[
            pl.BlockSpec((rows, a_width), lambda s, t: (row(s, t), 0)),
            pl.BlockSpec((rows, a_width), lambda s, t: (row(s, t), 1)),
            pl.BlockSpec((rows, kd), lambda s, t: (row(s, t), cq)),
            pl.BlockSpec((rows, kd), lambda s, t: (row(s, t), ck)),
            pl.BlockSpec((rows, vd), lambda s, t: (row(s, t), cv)),
            pl.BlockSpec((rows, vd), lambda s, t: (row(s, t), cr)),
            pl.BlockSpec((rows, kd), lambda s, t: (row(s, t), 0)),
            pl.BlockSpec((1, heads, dk, dv), lambda s, t: (s, 0, 0, 0)),
            const((1, dv)), const(w_s.shape), const(bias_full.shape),
            const((1, a_width)), const((1, a_width)),
        ],
        out_specs=out_specs,
        out_shape=out_shape,
        scratch_shapes=[pltpu.VMEM((heads, dk, dv), F32), pltpu.VMEM(w_s.shape, BF16)],
        compiler_params=_params("arbitrary", "arbitrary"),
        name="sgu_gla_mixers",
    )(proj, proj, proj, proj, proj, proj, b_all, s0, gn, w_s, bias_full, ln_g, ln_b)


def _mix_kernel(ap_ref, as_ref, bp_ref, bs_ref, xp_ref, xs_ref, w_ref, g_ref,
                x1_ref, xg_ref, ssq_ref, *, p_tiles):
    j = pl.program_id(1)
    ka = ap_ref.shape[1]

    def body(a_ref, b_ref, x_ref):
        half = x_ref.shape[0] // 2
        for rs in (slice(0, half), slice(half, 2 * half)):
            x1 = x_ref[rs, :] + (_dot(a_ref[rs, :], w_ref[:ka, :]) + _dot(b_ref[rs, :], w_ref[ka:, :]))
            x1_ref[rs, :] = x1
            xg_ref[rs, :] = (x1 * g_ref[...]).astype(xg_ref.dtype)
            sq = x1 * x1
            part = sq[:, :LANES]
            for c in range(1, sq.shape[1] // LANES):
                part = part + sq[:, c * LANES:(c + 1) * LANES]
            ssq_ref[rs, :] += part

    @pl.when(j == 0)
    def _():
        ssq_ref[...] = jnp.zeros_like(ssq_ref)

    _on_prompt_or_sample(pl.program_id(0), p_tiles, body,
                         (ap_ref, bp_ref, xp_ref), (as_ref, bs_ref, xs_ref))


def _mix(a_p, a_s, b_p, b_s, x_p, x_s, w_out, g):
    (n_p, ka), n_s = a_p.shape, a_s.shape[0]
    m = n_p + n_s
    kb = b_p.shape[1]
    n = w_out.shape[1]
    assert w_out.shape[0] == ka + kb
    bm = _pick(math.gcd(n_p, n_s), (1024, 512, 256, 128, 64))
    bn = _pick(n, (512, 256, 128))
    p_tiles = n_p // bm
    split = functools.partial(_split_specs, p_tiles=p_tiles, single_buffer_sample=True)
    return pl.pallas_call(
        functools.partial(_mix_kernel, p_tiles=p_tiles),
        grid=(m // bm, n // bn),
        in_specs=(split((bm, ka)) + split((bm, kb)) + split((bm, bn), col_tiles=n // bn) + [
            pl.BlockSpec((ka + kb, bn), lambda i, j: (0, j)),
            pl.BlockSpec((1, bn), lambda i, j: (0, j)),
        ]),
        out_specs=[
            pl.BlockSpec((bm, bn), lambda i, j: (i, j)),
            pl.BlockSpec((bm, bn), lambda i, j: (i, j)),
            pl.BlockSpec((bm, LANES), lambda i, j: (i, 0)),
        ],
        out_shape=[
            jax.ShapeDtypeStruct((m, n), F32),
            jax.ShapeDtypeStruct((m, n), BF16),
            jax.ShapeDtypeStruct((m, LANES), F32),
        ],
        compiler_params=_params("parallel", "arbitrary"),
        name="gemm_out_proj_residual",
    )(a_p, a_s, b_p, b_s, x_p, x_s, w_out, g)


def _ffn_up_kernel(xg_ref, ssq_ref, wg_hbm, wu_hbm, ride_hbm, o_ref, rode_hbm, wbuf, stage, sem,
                   ride_in, ride_out, ride_sem, *, d_model, d_ff):
    bn = o_ref.shape[1]
    n_steps = -(-d_ff // bn)
    edge = d_ff - (n_steps - 1) * bn
    edge_first = edge != bn
    ws = _WeightStream([wg_hbm, wu_hbm], wbuf, stage, sem, bn=bn, n_col_steps=n_steps,
                       first_col=n_steps - 1 if edge_first else 0, first_width=edge,
                       next_col=(lambda j: j) if edge_first else (lambda j: j + 1))
    rider = _CastRider(ride_hbm, rode_hbm, ride_in, ride_out, ride_sem,
                       ws.j * pl.num_programs(1) + ws.i)
    ws.advance()
    rider.advance()

    def body(width):
        ws.cast_arrived_chunk()
        rider.cast_arrived_chunk()
        rinv = lax.rsqrt(jnp.sum(ssq_ref[...], axis=-1, keepdims=True) * (1.0 / d_model) + EPS)
        h = xg_ref[...]
        g = _dot(h, ws.tile(0, width)) * rinv
        u = _dot(h, ws.tile(1, width)) * rinv
        o_ref[:, :width] = (_silu(g) * u).astype(o_ref.dtype)

    if edge_first:
        pl.when(ws.j == 0)(lambda: body(edge))
        pl.when(ws.j > 0)(lambda: body(bn))
    else:
        body(bn)


def _ffn_up(xg, ssq, wg, wu, ride):
    m, k = xg.shape
    f = wg.shape[1]
    bm = _pick(m, (1024, 512, 256, 128, 64))
    bn = 512
    assert (f % bn) % V7X_MXU_DIM == 0, "the edge column block must stay MXU-column aligned"
    n_steps = pl.cdiv(f, bn)
    n_row_steps = m // bm
    col = (lambda j: j) if f % bn == 0 else (lambda j: jnp.where(j == 0, n_steps - 1, j - 1))
    anywhere = pl.BlockSpec(memory_space=pl.ANY)
    return pl.pallas_call(
        functools.partial(_ffn_up_kernel, d_model=k, d_ff=f),
        grid=(n_steps, n_row_steps),
        in_specs=[
            pl.BlockSpec((bm, k), lambda j, i: (i, 0)),
            pl.BlockSpec((bm, LANES), lambda j, i: (i, 0)),
            anywhere, anywhere, anywhere,
        ],
        out_specs=[pl.BlockSpec((bm, bn), lambda j, i: (i, col(j))), anywhere],
        out_shape=[jax.ShapeDtypeStruct((m, f), BF16), jax.ShapeDtypeStruct(ride.shape, BF16)],
        scratch_shapes=(_weight_stream_scratch(2, k, bn, n_row_steps)
                        + _cast_rider_scratch(ride.shape, n_steps * n_row_steps)),
        compiler_params=_params("arbitrary", "arbitrary"),
        name="ffn_gate_up",
    )(xg, ssq, wg, wu, ride)


def _ffn_down_kernel(h_ref, w_ref, x_ref, o_ref, *, d_ff):
    k = pl.program_id(2)
    bk = h_ref.shape[1]
    n_steps = -(-d_ff // bk)
    last = d_ff - (n_steps - 1) * bk

    def body(width, first):
        p = _dot(h_ref[:, :width], w_ref[:width, :])
        if first:
            o_ref[...] = x_ref[...] + p
        else:
            o_ref[...] += p

    if n_steps == 1:
        body(last, True)
        return
    pl.when(k == 0)(lambda: body(bk, True))
    if n_steps > 2:
        pl.when((k > 0) & (k < n_steps - 1))(lambda: body(bk, False))
    pl.when(k == n_steps - 1)(lambda: body(last, False))


def _ffn_down(h1, wd, x1):
    m, f = h1.shape
    n = wd.shape[1]
    bm = _pick(m, (1024, 512, 256, 128, 64))
    bn = _pick(n, (1024, 512, 256, 128))
    bk = 2816
    assert (f % bk) % V7X_MXU_DIM == 0, "the last contraction block must stay MXU-row aligned"
    return pl.pallas_call(
        functools.partial(_ffn_down_kernel, d_ff=f),
        grid=(m // bm, n // bn, pl.cdiv(f, bk)),
        in_specs=[
            pl.BlockSpec((bm, bk), lambda i, j, k: (i, k)),
            pl.BlockSpec((bk, bn), lambda i, j, k: (k, j)),
            pl.BlockSpec((bm, bn), lambda i, j, k: (i, j)),
        ],
        out_specs=pl.BlockSpec((bm, bn), lambda i, j, k: (i, j)),
        out_shape=jax.ShapeDtypeStruct((m, n), F32),
        compiler_params=_params("parallel", "parallel", "arbitrary"),
        name="ffn_down_residual",
    )(h1, wd, x1)


def _ffn_down_norm_kernel(h_ref, w_ref, x_ref, g_ref, y_ref, *, d_ff):
    k = pl.program_id(1)
    bk = h_ref.shape[1]
    n_steps = -(-d_ff // bk)
    last = d_ff - (n_steps - 1) * bk

    def body(width, first, final):
        p = _dot(h_ref[:, :width], w_ref[:width, :])
        if first:
            y_ref[...] = x_ref[...] + p
        else:
            y_ref[...] += p
        if final:
            y_ref[...] = _rms(y_ref[...], g_ref[...])

    if n_steps == 1:
        body(last, True, True)
        return
    pl.when(k == 0)(lambda: body(bk, True, False))
    if n_steps > 2:
        pl.when((k > 0) & (k < n_steps - 1))(lambda: body(bk, False, False))
    pl.when(k == n_steps - 1)(lambda: body(last, False, True))


def _ffn_down_norm(h1, wd, x1, g, row0, n_rows):
    f = h1.shape[1]
    n = wd.shape[1]
    bm = _pick(math.gcd(n_rows, row0) if row0 else n_rows, (512, 256, 128, 64))
    bk = 1024
    assert (f % bk) % V7X_MXU_DIM == 0, "the last contraction block must stay MXU-row aligned"
    r0 = row0 // bm
    return pl.pallas_call(
        functools.partial(_ffn_down_norm_kernel, d_ff=f),
        grid=(n_rows // bm, pl.cdiv(f, bk)),
        in_specs=[
            pl.BlockSpec((bm, bk), lambda i, k: (r0 + i, k)),
            pl.BlockSpec((bk, n), lambda i, k: (k, 0)),
            pl.BlockSpec((bm, n), lambda i, k: (r0 + i, 0)),
            pl.BlockSpec((1, n), lambda i, k: (0, 0)),
        ],
        out_specs=pl.BlockSpec((bm, n), lambda i, k: (i, 0)),
        out_shape=jax.ShapeDtypeStruct((n_rows, n), F32),
        compiler_params=_params("parallel", "arbitrary"),
        name="ffn_down_final_norm",
    )(h1, wd, x1, g)


def kernel(x_prompt, x_sample, state_gla, g_mix, w_in, w_s, b_s, ln_g, ln_b, w_gate_up, b_gate,
           gla_norm_g, w_out, g_ffn, w_ffn_gate, w_ffn_up, w_ffn_down, g_final):
    depth = g_mix.shape[0]
    n_p, len_p, d_model = x_prompt.shape
    n_s, len_s, _ = x_sample.shape
    rows_p, rows_s = n_p * len_p, n_s * len_s
    _, _, heads_b, dk, dv = state_gla.shape
    a_width = ln_g.shape[-1]
    a_heads, sgu_chunk = w_s.shape[1], w_s.shape[2]
    key_dim, val_dim = heads_b * dk, heads_b * dv
    rank = w_gate_up.shape[1]
    d_ff = w_ffn_gate.shape[-1]
    main_cols = 2 * a_width + 2 * key_dim + 2 * val_dim
    col_q = 2 * a_width
    col_k = col_q + key_dim
    col_v = col_k + key_dim
    col_r = col_v + val_dim
    chunk_p = min(sgu_chunk, len_p)
    chunk_s = min(sgu_chunk, len_s)

    x_p = x_prompt.reshape(rows_p, d_model)
    x_s = x_sample.reshape(rows_s, d_model)
    zero_state = jnp.zeros((n_p, heads_b, dk, dv), state_gla.dtype)

    gla_p, gla_s, vn_s = [], [], []
    for d in range(depth):
        w_up = jnp.pad(w_gate_up[d], ((0, LANES - rank), (0, 0))).astype(BF16)
        head_dim = a_width // a_heads
        bias_full = jnp.repeat(b_s[d].T, head_dim, axis=1)
        row = lambda v: v.reshape(1, -1)

        w_in_t = w_in[d].T
        h, b_all = _prep(x_p, x_s, row(g_mix[d]), w_in_t, main_cols, rank, w_up, row(b_gate[d]))
        q_scale = dk ** -0.5
        proj, w_o = _in_proj(h, w_in_t, main_cols, (
            (0, col_q, _gelu_tanh),
            (col_q, col_k, lambda t: t * q_scale),
            (col_k, col_r, lambda t: t),
            (col_r, main_cols, _silu),
        ), ride=w_out[d])

        cols = dict(col_q=col_q, col_k=col_k, col_v=col_v, col_r=col_r)
        shared = (row(gla_norm_g[d]),)
        ln = (row(ln_g[d]), row(ln_b[d]))
        a_p, o_p, s_p = _mixers(
            proj, b_all, zero_state, *shared, w_s[d][:, :chunk_p, :chunk_p], bias_full[:chunk_p],
            *ln, 0, n_p, len_p, **cols, sgu_chunk=chunk_p, emit_vn=False)
        a_s, o_s, s_s, vn = _mixers(
            proj, b_all, state_gla[d], *shared, w_s[d][:, :chunk_s, :chunk_s], bias_full[:chunk_s],
            *ln, rows_p, n_s, len_s, **cols, sgu_chunk=chunk_s, emit_vn=True)

        x1, xg, ssq = _mix(a_p, a_s, o_p, o_s, x_p, x_s, w_o, row(g_ffn[d]))
        h1, wd = _ffn_up(xg, ssq, w_ffn_gate[d], w_ffn_up[d], ride=w_ffn_down[d])
        if d + 1 < depth:
            x = _ffn_down(h1, wd, x1)
            x_p, x_s = x[:rows_p], x[rows_p:]
        else:
            gf = g_final.reshape(1, -1)
            y_p = _ffn_down_norm(h1, wd, x1, gf, 0, rows_p).reshape(n_p, len_p, d_model)
            y_s = _ffn_down_norm(h1, wd, x1, gf, rows_p, rows_s).reshape(n_s, len_s, d_model)

        gla_p.append(s_p)
        gla_s.append(s_s)
        vn_s.append(vn.reshape(n_s, len_s, a_width))

    return (y_p, y_s, jnp.stack(gla_p), jnp.stack(gla_s), jnp.stack(vn_s))
```

```python
import functools
import math

import jax
import jax.numpy as jnp
from jax import lax
from jax.experimental import pallas as pl
from jax.experimental.pallas import tpu as pltpu

F32 = jnp.float32
BF16 = jnp.bfloat16

EPS = 1e-6
GLA_BLOCK = 64
GATE_TAU = 16.0
LANES = 128
BF16_SUBLANES = 16
V7X_MXU_DIM = 256
V7X_VMEM_LIMIT_BYTES = 56 * 1024 * 1024


def _pick(n, candidates):
    for c in candidates:
        if n % c == 0:
            return c
    raise ValueError(f"no tile in {candidates} divides {n}")


def _params(*semantics):
    return pltpu.CompilerParams(dimension_semantics=semantics,
                                vmem_limit_bytes=V7X_VMEM_LIMIT_BYTES)


def _dot(a, b):
    return jnp.dot(a, b, preferred_element_type=F32)


def _gelu_tanh(x):
    c = math.sqrt(2.0 / math.pi)
    return 0.5 * x * (1.0 + jnp.tanh(c * (x + 0.044715 * (x * x * x))))


def _silu(x):
    return x * (1.0 / (1.0 + jnp.exp(-x)))


def _rms(x, g):
    return x * lax.rsqrt(jnp.mean(x * x, axis=-1, keepdims=True) + EPS) * g


def _prep_kernel(xp_ref, xs_ref, g_ref, win_ref, wup_ref, bg_ref, tri_ref, h_ref, b_ref, wlr_ref,
                 *, p_tiles, sub, rank):
    @pl.when(pl.program_id(0) == 0)
    def _():
        pad = jnp.zeros((LANES - rank, win_ref.shape[1]), F32)
        wlr_ref[...] = jnp.transpose(jnp.concatenate([win_ref[...], pad], axis=0)).astype(BF16)

    def body(x_ref):
        for c in range(x_ref.shape[0] // sub):
            rs = slice(c * sub, (c + 1) * sub)
            hb = _rms(x_ref[rs, :], g_ref[...]).astype(BF16)
            h_ref[rs, :] = hb
            g_lr = _dot(hb, wlr_ref[...])
            z = _dot(g_lr.astype(BF16), wup_ref[...]) + bg_ref[...]
            log_a = (jnp.minimum(z, 0.0) - jnp.log1p(jnp.exp(-jnp.abs(z)))) * (1.0 / GATE_TAU)
            hi = log_a.astype(BF16)
            r1 = log_a - hi.astype(F32)
            mid = r1.astype(BF16)
            lo = (r1 - mid.astype(F32)).astype(BF16)
            tri = tri_ref[...]
            b_ref[rs, :] = _dot(tri, hi) + _dot(tri, mid) + _dot(tri, lo)

    _on_prompt_or_sample(pl.program_id(0), p_tiles, body, (xp_ref,), (xs_ref,))


def _on_prompt_or_sample(i, p_tiles, body, prompt_refs, sample_refs):
    @pl.when(i < p_tiles)
    def _():
        body(*prompt_refs)

    @pl.when(i >= p_tiles)
    def _():
        body(*sample_refs)


def _split_specs(block, p_tiles, col_tiles=None, single_buffer_sample=False):
    def prompt_map(i, *g):
        col = jnp.where(i < p_tiles, g[0], col_tiles - 1) if col_tiles else 0
        return jnp.minimum(i, p_tiles - 1), col

    def sample_map(i, *g):
        col = jnp.where(i >= p_tiles, g[0], 0) if col_tiles else 0
        return jnp.maximum(i - p_tiles, 0), col

    mode = dict(pipeline_mode=pl.Buffered(1)) if single_buffer_sample else {}
    return [pl.BlockSpec(block, prompt_map), pl.BlockSpec(block, sample_map, **mode)]


def _prep(x_p, x_s, g, w_in_t, gate_row0, rank, w_up, b_gate):
    (n_p, d), n_s = x_p.shape, x_s.shape[0]
    n = n_p + n_s
    key_dim = w_up.shape[1]
    assert gate_row0 % rank == 0 and rank % 8 == 0 and rank <= LANES
    bm = _pick(math.gcd(n_p, n_s), (512, 256, 128, 64))
    sub = _pick(bm, (128, 64))
    idx = jnp.arange(sub)
    tri = ((idx[None, :] <= idx[:, None])
           & (idx[None, :] // GLA_BLOCK == idx[:, None] // GLA_BLOCK)).astype(BF16)
    p_tiles = n_p // bm
    const = lambda shape: pl.BlockSpec(shape, lambda i: (0, 0))
    return pl.pallas_call(
        functools.partial(_prep_kernel, p_tiles=p_tiles, sub=sub, rank=rank),
        grid=(n // bm,),
        in_specs=_split_specs((bm, d), p_tiles, single_buffer_sample=True) + [
            const((1, d)),
            pl.BlockSpec((rank, d), lambda i: (gate_row0 // rank, 0)),
            const(w_up.shape), const((1, key_dim)), const((sub, sub)),
        ],
        out_specs=[
            pl.BlockSpec((bm, d), lambda i: (i, 0)),
            pl.BlockSpec((bm, key_dim), lambda i: (i, 0)),
        ],
        out_shape=[
            jax.ShapeDtypeStruct((n, d), BF16),
            jax.ShapeDtypeStruct((n, key_dim), F32),
        ],
        scratch_shapes=[pltpu.VMEM((d, LANES), BF16)],
        compiler_params=_params("arbitrary"),
        name="prep_rmsnorm_gate",
    )(x_p, x_s, g, w_in_t, w_up, b_gate, tri)


class _WeightStream:
    def __init__(self, w_hbms, wbuf, stage, sem, *, bn, n_col_steps, first_col, first_width,
                 next_col, transposed=False):
        self.w_hbms, self.wbuf, self.stage, self.sem = w_hbms, wbuf, stage, sem
        self.transposed = transposed
        self.bn = bn
        self.rows = stage.shape[2]
        self.tile_rows = wbuf.shape[2] - self.rows
        self.n_chunks = self.tile_rows // self.rows
        self.n_col_steps = n_col_steps
        self.first_col, self.first_width = first_col, first_width
        self.next_col = next_col
        self.j = pl.program_id(0)
        self.i = pl.program_id(1)

    def _copy(self, m, chunk, col0, width, slot):
        if self.transposed:
            assert width == self.bn
            src = self.w_hbms[m].at[pl.ds(col0 + chunk * self.rows, self.rows), :]
            dst = self.stage.at[m, slot]
        else:
            src = self.w_hbms[m].at[pl.ds(chunk * self.rows, self.rows), pl.ds(col0, width)]
            dst = self.stage.at[m, slot, :, pl.ds(0, width)]
        return pltpu.make_async_copy(src, dst, self.sem.at[m, slot])

    def _next_copy(self, m, chunk):
        col0 = pl.multiple_of(self.next_col(self.j) * self.bn, self.bn)
        return self._copy(m, chunk, col0, self.bn, chunk % 2)

    def advance(self):
        j, i, n_mats = self.j, self.i, len(self.w_hbms)

        @pl.when((j == 0) & (i == 0))
        def _():
            self.stage[...] = jnp.zeros_like(self.stage)
            for c in range(self.n_chunks):
                for m in range(n_mats):
                    cp = self._copy(m, c, self.first_col * self.bn, self.first_width, c % 2)
                    cp.start()
                    cp.wait()
                    self._store_chunk(m, 0, c, c % 2)

        has_next = j + 1 < self.n_col_steps

        @pl.when(has_next & (i >= 1) & (i <= self.n_chunks))
        def _():
            for m in range(n_mats):
                self._next_copy(m, i - 1).wait()

        @pl.when(has_next & (i < self.n_chunks))
        def _():
            for m in range(n_mats):
                self._next_copy(m, i).start()

    def _store_chunk(self, m, tile_slot, chunk, stage_slot):
        row0 = chunk * self.rows
        if not isinstance(row0, int):
            row0 = pl.multiple_of(row0, self.rows)
        self.wbuf[m, tile_slot, pl.ds(row0, self.rows), :] = self.stage[m, stage_slot].astype(BF16)

    def cast_arrived_chunk(self):
        j, i = self.j, self.i
        valid = (j + 1 < self.n_col_steps) & (i >= 1) & (i <= self.n_chunks)
        chunk = jnp.where(valid, i - 1, self.n_chunks)
        for m in range(len(self.w_hbms)):
            self._store_chunk(m, (j + 1) % 2, chunk, (i + 1) % 2)

    def tile(self, m, width=None):
        w = self.wbuf[m, self.j % 2, :self.tile_rows, :]
        return w if width is None else w[:, :width]


class _CastRider:
    def __init__(self, src_hbm, dst_hbm, in_buf, out_buf, sem, step):
        self.src, self.dst, self.in_buf, self.out_buf, self.sem = src_hbm, dst_hbm, in_buf, out_buf, sem
        self.rows = in_buf.shape[1]
        self.n_chunks = src_hbm.shape[0] // self.rows
        self.step = step

    def _fetch(self, chunk):
        rs = pl.ds(pl.multiple_of(chunk * self.rows, self.rows), self.rows)
        return pltpu.make_async_copy(self.src.at[rs, :], self.in_buf.at[chunk % 2],
                                     self.sem.at[0, chunk % 2])

    def _write_back(self, chunk):
        rs = pl.ds(pl.multiple_of(chunk * self.rows, self.rows), self.rows)
        return pltpu.make_async_copy(self.out_buf.at[chunk % 2], self.dst.at[rs, :],
                                     self.sem.at[1, chunk % 2])

    def advance(self):
        g, n = self.step, self.n_chunks

        @pl.when(g == 0)
        def _():
            self.in_buf[...] = jnp.zeros_like(self.in_buf)

        steady = (g >= 3) & (g < n)

        @pl.when(steady)
        def _():
            self._write_back(g - 3).wait()
            self._fetch(g - 1).wait()
            self._fetch(g).start()
            self._write_back(g - 2).start()

        @pl.when(jnp.logical_not(steady))
        def _():
            pl.when((g >= 3) & (g < n + 3))(lambda: self._write_back(g - 3).wait())
            pl.when((g >= 1) & (g < n + 1))(lambda: self._fetch(g - 1).wait())
            pl.when(g < n)(lambda: self._fetch(g).start())
            pl.when((g >= 2) & (g < n + 2))(lambda: self._write_back(g - 2).start())

    def cast_arrived_chunk(self):
        slot = (self.step + 1) % 2
        self.out_buf[slot] = self.in_buf[slot].astype(BF16)


def _cast_rider_scratch(src_shape, n_steps):
    n_rows, n_cols = src_shape
    rows = BF16_SUBLANES
    while n_rows % rows or n_rows // rows + 3 > n_steps:
        rows += BF16_SUBLANES
        assert rows <= n_rows, "too few grid steps to ride the cast on"
    return [
        pltpu.VMEM((2, rows, n_cols), F32),
        pltpu.VMEM((2, rows, n_cols), BF16),
        pltpu.SemaphoreType.DMA((2, 2)),
    ]


_MAX_WEIGHT_CHUNKS = 16


def _weight_stream_scratch(n_mats, k, bn, n_row_steps, transposed=False):
    assert n_row_steps >= 2, "weight chunks are prefetched across the row steps of one column tile"
    tile_rows, tile_cols = (bn, k) if transposed else (k, bn)
    n_chunks = 1
    while (n_chunks * 2 <= min(n_row_steps - 1, _MAX_WEIGHT_CHUNKS)
           and tile_rows % (n_chunks * 2 * BF16_SUBLANES) == 0):
        n_chunks *= 2
    rows = tile_rows // n_chunks
    return [
        pltpu.VMEM((n_mats, 2, tile_rows + rows, tile_cols), BF16),
        pltpu.VMEM((n_mats, 2, rows, tile_cols), F32),
        pltpu.SemaphoreType.DMA((n_mats, 2)),
    ]


def _in_proj_kernel(a_ref, wt_hbm, ride_hbm, o_ref, rode_hbm, wbuf, stage, sem,
                    ride_in, ride_out, ride_sem, *, epilogues, n_col_steps):
    bn = o_ref.shape[1]
    ws = _WeightStream([wt_hbm], wbuf, stage, sem, bn=bn, n_col_steps=n_col_steps,
                       first_col=0, first_width=bn, next_col=lambda j: j + 1, transposed=True)
    rider = _CastRider(ride_hbm, rode_hbm, ride_in, ride_out, ride_sem,
                       ws.j * pl.num_programs(1) + ws.i)
    ws.advance()
    rider.advance()
    for first, end, fn in epilogues:
        @pl.when((ws.j >= first) & (ws.j < end))
        def _(fn=fn):
            ws.cast_arrived_chunk()
            rider.cast_arrived_chunk()
            acc = lax.dot_general(a_ref[...], ws.tile(0), (((1,), (1,)), ((), ())),
                                  preferred_element_type=F32)
            o_ref[...] = fn(acc).astype(o_ref.dtype)


def _in_proj(h, w_t, n_cols, col_epilogues, ride):
    m, k = h.shape
    bm = _pick(m, (1024, 512, 256, 128, 64))
    bn = _pick(math.gcd(*[end for _, end, _ in col_epilogues]), (1024, 512, 256, 128))
    epilogues = tuple((first // bn, end // bn, fn) for first, end, fn in col_epilogues)
    n_col_steps = n_cols // bn
    n_row_steps = m // bm
    anywhere = pl.BlockSpec(memory_space=pl.ANY)
    return pl.pallas_call(
        functools.partial(_in_proj_kernel, epilogues=epilogues, n_col_steps=n_col_steps),
        grid=(n_col_steps, n_row_steps),
        in_specs=[pl.BlockSpec((bm, k), lambda j, i: (i, 0)), anywhere, anywhere],
        out_specs=[pl.BlockSpec((bm, bn), lambda j, i: (i, j)), anywhere],
        out_shape=[jax.ShapeDtypeStruct((m, n_cols), BF16),
                   jax.ShapeDtypeStruct(ride.shape, BF16)],
        scratch_shapes=(_weight_stream_scratch(1, k, bn, n_row_steps, transposed=True)
                        + _cast_rider_scratch(ride.shape, n_col_steps * n_row_steps)),
        compiler_params=_params("arbitrary", "arbitrary"),
        name="gemm_in_proj",
    )(h, w_t, ride)


def _sgu_tile(u_ref, v_ref, wm_ref, bias_ref, lng_ref, lnb_ref, o_ref, vn_ref):
    rows, width = u_ref.shape
    heads, chunk, _ = wm_ref.shape
    head_dim = width // heads
    n_chunks = rows // chunk
    v = v_ref[...].astype(F32)
    mu = jnp.mean(v, axis=-1, keepdims=True)
    vc = v - mu
    var = jnp.mean(vc * vc, axis=-1, keepdims=True)
    vn = vc * lax.rsqrt(var + EPS) * lng_ref[...] + lnb_ref[...]
    if vn_ref is not None:
        vn_ref[...] = vn
    vnb = vn.astype(BF16)
    for h in range(heads):
        lo, hi = h * head_dim, (h + 1) * head_dim
        rhs = jnp.concatenate(
            [vnb[c * chunk:(c + 1) * chunk, lo:hi] for c in range(n_chunks)], axis=1)
        mixed = _dot(wm_ref[h], rhs)
        bias = bias_ref[:, lo:hi]
        for c in range(n_chunks):
            rs = slice(c * chunk, (c + 1) * chunk)
            u = u_ref[rs, lo:hi].astype(F32)
            m = mixed[:, c * head_dim:(c + 1) * head_dim] + bias
            o_ref[rs, lo:hi] = (u * m).astype(o_ref.dtype)


def _mixers_kernel(p_ref, b_ref, s0_ref, gn_ref,
                   w_ref, bias_ref, lng_ref, lnb_ref, a_ref, o_ref, s_out_ref, *rest,
                   heads, dk, dv, blocks, emit_vn, cols):
    vn_ref, s_ref, wm_ref = rest if emit_vn else (None,) + rest
    t = pl.program_id(1)
    u_ref, vg_ref, q_ref, k_ref, v_ref, r_ref = (
        p_ref.at[:, c0:c0 + w] for c0, w in (cols[n] for n in ("u", "vg", "q", "k", "v", "r")))

    @pl.when((pl.program_id(0) == 0) & (t == 0))
    def _():
        chunk = w_ref.shape[1]
        ri = lax.broadcasted_iota(jnp.int32, (chunk, chunk), 0)
        ci = lax.broadcasted_iota(jnp.int32, (chunk, chunk), 1)
        for h in range(w_ref.shape[0]):
            wm_ref[h] = jnp.where(ci <= ri, w_ref[h], 0.0).astype(BF16)

    @pl.when(t == 0)
    def _():
        s_ref[...] = s0_ref[0]

    _sgu_tile(u_ref, vg_ref, wm_ref, bias_ref, lng_ref, lnb_ref, a_ref, vn_ref)

    ri = lax.broadcasted_iota(jnp.int32, (GLA_BLOCK, GLA_BLOCK), 0)
    ci = lax.broadcasted_iota(jnp.int32, (GLA_BLOCK, GLA_BLOCK), 1)
    causal = ci <= ri
    half = GLA_BLOCK // 2
    for blk in range(blocks):
        rs = slice(blk * GLA_BLOCK, (blk + 1) * GLA_BLOCK)
        for h in range(heads):
            ks = slice(h * dk, (h + 1) * dk)
            vs = slice(h * dv, (h + 1) * dv)
            q = q_ref[rs, ks].astype(F32)
            k = k_ref[rs, ks].astype(F32)
            v = v_ref[rs, vs]
            b = b_ref[rs, ks]
            b_mid = b[half:half + 1, :]
            b_last = b[GLA_BLOCK - 1:GLA_BLOCK, :]
            q_in = (q * jnp.exp(b - b_mid)).astype(BF16)
            k_in = (k * jnp.exp(b_mid - b)).astype(BF16)
            att = lax.dot_general(q_in, k_in, (((1,), (1,)), ((), ())),
                                  preferred_element_type=F32)
            att = jnp.where(causal, att, 0.0).astype(BF16)
            s = s_ref[h]
            o = _dot(att, v) + _dot((q * jnp.exp(b)).astype(BF16), s.astype(BF16))
            k_out = (k * jnp.exp(b_last - b)).astype(BF16)
            kv = lax.dot_general(k_out, v, (((0,), (0,)), ((), ())),
                                 preferred_element_type=F32)
            decay_col = jnp.transpose(
                jnp.broadcast_to(jnp.exp(b_last), (LANES, dk)))
            decay = jnp.concatenate([decay_col] * (dv // LANES), axis=1)
            s_ref[h] = decay * s + kv
            o_n = _rms(o, gn_ref[...])
            o_ref[rs, vs] = (o_n * r_ref[rs, vs].astype(F32)).astype(o_ref.dtype)

    @pl.when(t == pl.num_programs(1) - 1)
    def _():
        s_out_ref[0] = s_ref[...]


def _mixers(proj, b_all, s0, gn, w_s, bias_full, ln_g, ln_b, row0, n_streams, stream_len, *,
            col_q, col_k, col_v, col_r, sgu_chunk, emit_vn):
    _, heads, dk, dv = s0.shape
    kd, vd = heads * dk, heads * dv
    a_width = ln_g.shape[1]
    blocks = _pick(stream_len // GLA_BLOCK, (4, 2, 1))
    rows = blocks * GLA_BLOCK
    assert rows % sgu_chunk == 0, "an SGU chunk may not straddle two row tiles"
    steps = stream_len // rows
    r0 = row0 // rows
    assert w_s.shape[1:] == (sgu_chunk, sgu_chunk)
    cols = dict(u=(0, a_width), vg=(a_width, a_width), q=(col_q, kd), k=(col_k, kd),
                v=(col_v, vd), r=(col_r, vd))
    kern = functools.partial(_mixers_kernel, heads=heads, dk=dk, dv=dv, blocks=blocks,
                             emit_vn=emit_vn, cols=cols)
    row = lambda s, t: r0 + s * steps + t
    out_row = lambda s, t: s * steps + t
    n_rows = n_streams * stream_len
    const = lambda shape: pl.BlockSpec(shape, lambda s, t: (0,) * len(shape))
    out_specs = [
        pl.BlockSpec((rows, a_width), lambda s, t: (out_row(s, t), 0)),
        pl.BlockSpec((rows, vd), lambda s, t: (out_row(s, t), 0)),
        pl.BlockSpec((1, heads, dk, dv), lambda s, t: (s, 0, 0, 0)),
    ]
    out_shape = [
        jax.ShapeDtypeStruct((n_rows, a_width), BF16),
        jax.ShapeDtypeStruct((n_rows, vd), BF16),
        jax.ShapeDtypeStruct(s0.shape, F32),
    ]
    if emit_vn:
        out_specs.append(pl.BlockSpec((rows, a_width), lambda s, t: (out_row(s, t), 0)))
        out_shape.append(jax.ShapeDtypeStruct((n_rows, a_width), F32))
    return pl.pallas_call(
        kern,
        grid=(n_streams, steps),
        in_specs=[
            pl.BlockSpec((rows, proj.shape[1]), lambda s, t: (row(s, t), 0)),
            pl.BlockSpec((rows, kd), lambda s, t: (row(s, t), 0)),
            pl.BlockSpec((1, heads, dk, dv), lambda s, t: (s, 0, 0, 0)),
            const((1, dv)), const(w_s.shape), const(bias_full.shape),
            const((1, a_width)), const((1, a_width)),
        ],
        out_specs=out_specs,
        out_shape=out_shape,
        scratch_shapes=[pltpu.VMEM((heads, dk, dv), F32), pltpu.VMEM(w_s.shape, BF16)],
        compiler_params=_params("arbitrary", "arbitrary"),
        name="sgu_gla_mixers",
    )(proj, b_all, s0, gn, w_s, bias_full, ln_g, ln_b)


def _mix_kernel(ap_ref, as_ref, bp_ref, bs_ref, xp_ref, xs_ref, wa_ref, wb_ref, g_ref,
                x1_ref, xg_ref, ssq_ref, *, p_tiles):
    j = pl.program_id(1)

    def body(a_ref, b_ref, x_ref):
        half = x_ref.shape[0] // 2
        for rs in (slice(0, half), slice(half, 2 * half)):
            x1 = x_ref[rs, :] + (_dot(a_ref[rs, :], wa_ref[...]) + _dot(b_ref[rs, :], wb_ref[...]))
            x1_ref[rs, :] = x1
            xg_ref[rs, :] = (x1 * g_ref[...]).astype(xg_ref.dtype)
            sq = x1 * x1
            part = sq[:, :LANES]
            for c in range(1, sq.shape[1] // LANES):
                part = part + sq[:, c * LANES:(c + 1) * LANES]
            ssq_ref[rs, :] += part

    @pl.when(j == 0)
    def _():
        ssq_ref[...] = jnp.zeros_like(ssq_ref)

    _on_prompt_or_sample(pl.program_id(0), p_tiles, body,
                         (ap_ref, bp_ref, xp_ref), (as_ref, bs_ref, xs_ref))


def _mix(a_p, a_s, b_p, b_s, x_p, x_s, w_out, g):
    (n_p, ka), n_s = a_p.shape, a_s.shape[0]
    m = n_p + n_s
    kb = b_p.shape[1]
    n = w_out.shape[1]
    assert ka == kb, "head groups of different widths need separate weight specs"
    bm = _pick(math.gcd(n_p, n_s), (1024, 512, 256, 128, 64))
    bn = _pick(n, (512, 256, 128))
    p_tiles = n_p // bm
    split = functools.partial(_split_specs, p_tiles=p_tiles, single_buffer_sample=True)
    return pl.pallas_call(
        functools.partial(_mix_kernel, p_tiles=p_tiles),
        grid=(m // bm, n // bn),
        in_specs=(split((bm, ka)) + split((bm, kb)) + split((bm, bn), col_tiles=n // bn) + [
            pl.BlockSpec((ka, bn), lambda i, j: (0, j)),
            pl.BlockSpec((kb, bn), lambda i, j: (1, j)),
            pl.BlockSpec((1, bn), lambda i, j: (0, j)),
        ]),
        out_specs=[
            pl.BlockSpec((bm, bn), lambda i, j: (i, j)),
            pl.BlockSpec((bm, bn), lambda i, j: (i, j)),
            pl.BlockSpec((bm, LANES), lambda i, j: (i, 0)),
        ],
        out_shape=[
            jax.ShapeDtypeStruct((m, n), F32),
            jax.ShapeDtypeStruct((m, n), BF16),
            jax.ShapeDtypeStruct((m, LANES), F32),
        ],
        compiler_params=_params("parallel", "arbitrary"),
        name="gemm_out_proj_residual",
    )(a_p, a_s, b_p, b_s, x_p, x_s, w_out, w_out, g)


def _ffn_up_kernel(xg_ref, ssq_ref, wg_hbm, wu_hbm, ride_hbm, o_ref, rode_hbm, wbuf, stage, sem,
                   ride_in, ride_out, ride_sem, *, d_model, d_ff):
    bn = o_ref.shape[1]
    n_steps = -(-d_ff // bn)
    edge = d_ff - (n_steps - 1) * bn
    edge_first = edge != bn
    ws = _WeightStream([wg_hbm, wu_hbm], wbuf, stage, sem, bn=bn, n_col_steps=n_steps,
                       first_col=n_steps - 1 if edge_first else 0, first_width=edge,
                       next_col=(lambda j: j) if edge_first else (lambda j: j + 1))
    rider = _CastRider(ride_hbm, rode_hbm, ride_in, ride_out, ride_sem,
                       ws.j * pl.num_programs(1) + ws.i)
    ws.advance()
    rider.advance()

    def body(width):
        ws.cast_arrived_chunk()
        rider.cast_arrived_chunk()
        rinv = lax.rsqrt(jnp.sum(ssq_ref[...], axis=-1, keepdims=True) * (1.0 / d_model) + EPS)
        h = xg_ref[...]
        g = _dot(h, ws.tile(0, width)) * rinv
        u = _dot(h, ws.tile(1, width)) * rinv
        o_ref[:, :width] = (_silu(g) * u).astype(o_ref.dtype)

    if edge_first:
        pl.when(ws.j == 0)(lambda: body(edge))
        pl.when(ws.j > 0)(lambda: body(bn))
    else:
        body(bn)


def _ffn_up(xg, ssq, wg, wu, ride):
    m, k = xg.shape
    f = wg.shape[1]
    bm = _pick(m, (1024, 512, 256, 128, 64))
    bn = 512
    assert (f % bn) % V7X_MXU_DIM == 0, "the edge column block must stay MXU-column aligned"
    n_steps = pl.cdiv(f, bn)
    n_row_steps = m // bm
    col = (lambda j: j) if f % bn == 0 else (lambda j: jnp.where(j == 0, n_steps - 1, j - 1))
    anywhere = pl.BlockSpec(memory_space=pl.ANY)
    return pl.pallas_call(
        functools.partial(_ffn_up_kernel, d_model=k, d_ff=f),
        grid=(n_steps, n_row_steps),
        in_specs=[
            pl.BlockSpec((bm, k), lambda j, i: (i, 0)),
            pl.BlockSpec((bm, LANES), lambda j, i: (i, 0)),
            anywhere, anywhere, anywhere,
        ],
        out_specs=[pl.BlockSpec((bm, bn), lambda j, i: (i, col(j))), anywhere],
        out_shape=[jax.ShapeDtypeStruct((m, f), BF16), jax.ShapeDtypeStruct(ride.shape, BF16)],
        scratch_shapes=(_weight_stream_scratch(2, k, bn, n_row_steps)
                        + _cast_rider_scratch(ride.shape, n_steps * n_row_steps)),
        compiler_params=_params("arbitrary", "arbitrary"),
        name="ffn_gate_up",
    )(xg, ssq, wg, wu, ride)


def _ffn_down_kernel(h_ref, w_ref, x_ref, o_ref, *, d_ff):
    k = pl.program_id(2)
    bk = h_ref.shape[1]
    n_steps = -(-d_ff // bk)
    last = d_ff - (n_steps - 1) * bk

    def body(width, first):
        p = _dot(h_ref[:, :width], w_ref[:width, :])
        if first:
            o_ref[...] = x_ref[...] + p
        else:
            o_ref[...] += p

    if n_steps == 1:
        body(last, True)
        return
    pl.when(k == 0)(lambda: body(bk, True))
    if n_steps > 2:
        pl.when((k > 0) & (k < n_steps - 1))(lambda: body(bk, False))
    pl.when(k == n_steps - 1)(lambda: body(last, False))


def _ffn_down(h1, wd, x1):
    m, f = h1.shape
    n = wd.shape[1]
    bm = _pick(m, (1024, 512, 256, 128, 64))
    bn = _pick(n, (1024, 512, 256, 128))
    bk = 2816
    assert (f % bk) % V7X_MXU_DIM == 0, "the last contraction block must stay MXU-row aligned"
    return pl.pallas_call(
        functools.partial(_ffn_down_kernel, d_ff=f),
        grid=(m // bm, n // bn, pl.cdiv(f, bk)),
        in_specs=[
            pl.BlockSpec((bm, bk), lambda i, j, k: (i, k)),
            pl.BlockSpec((bk, bn), lambda i, j, k: (k, j)),
            pl.BlockSpec((bm, bn), lambda i, j, k: (i, j)),
        ],
        out_specs=pl.BlockSpec((bm, bn), lambda i, j, k: (i, j)),
        out_shape=jax.ShapeDtypeStruct((m, n), F32),
        compiler_params=_params("parallel", "parallel", "arbitrary"),
        name="ffn_down_residual",
    )(h1, wd, x1)


def _ffn_down_norm_kernel(h_ref, w_ref, x_ref, g_ref, y_ref, *, d_ff):
    k = pl.program_id(1)
    bk = h_ref.shape[1]
    n_steps = -(-d_ff // bk)
    last = d_ff - (n_steps - 1) * bk

    def body(width, first, final):
        p = _dot(h_ref[:, :width], w_ref[:width, :])
        if first:
            y_ref[...] = x_ref[...] + p
        else:
            y_ref[...] += p
        if final:
            y_ref[...] = _rms(y_ref[...], g_ref[...])

    if n_steps == 1:
        body(last, True, True)
        return
    pl.when(k == 0)(lambda: body(bk, True, False))
    if n_steps > 2:
        pl.when((k > 0) & (k < n_steps - 1))(lambda: body(bk, False, False))
    pl.when(k == n_steps - 1)(lambda: body(last, False, True))


def _ffn_down_norm(h1, wd, x1, g, row0, n_rows):
    f = h1.shape[1]
    n = wd.shape[1]
    bm = _pick(math.gcd(n_rows, row0) if row0 else n_rows, (512, 256, 128, 64))
    bk = 1024
    assert (f % bk) % V7X_MXU_DIM == 0, "the last contraction block must stay MXU-row aligned"
    r0 = row0 // bm
    return pl.pallas_call(
        functools.partial(_ffn_down_norm_kernel, d_ff=f),
        grid=(n_rows // bm, pl.cdiv(f, bk)),
        in_specs=[
            pl.BlockSpec((bm, bk), lambda i, k: (r0 + i, k)),
            pl.BlockSpec((bk, n), lambda i, k: (k, 0)),
            pl.BlockSpec((bm, n), lambda i, k: (r0 + i, 0)),
            pl.BlockSpec((1, n), lambda i, k: (0, 0)),
        ],
        out_specs=pl.BlockSpec((bm, n), lambda i, k: (i, 0)),
        out_shape=jax.ShapeDtypeStruct((n_rows, n), F32),
        compiler_params=_params("parallel", "arbitrary"),
        name="ffn_down_final_norm",
    )(h1, wd, x1, g)


def kernel(x_prompt, x_sample, state_gla, g_mix, w_in, w_s, b_s, ln_g, ln_b, w_gate_up, b_gate,
           gla_norm_g, w_out, g_ffn, w_ffn_gate, w_ffn_up, w_ffn_down, g_final):
    depth = g_mix.shape[0]
    n_p, len_p, d_model = x_prompt.shape
    n_s, len_s, _ = x_sample.shape
    rows_p, rows_s = n_p * len_p, n_s * len_s
    _, _, heads_b, dk, dv = state_gla.shape
    a_width = ln_g.shape[-1]
    a_heads, sgu_chunk = w_s.shape[1], w_s.shape[2]
    key_dim, val_dim = heads_b * dk, heads_b * dv
    rank = w_gate_up.shape[1]
    d_ff = w_ffn_gate.shape[-1]
    main_cols = 2 * a_width + 2 * key_dim + 2 * val_dim
    col_q = 2 * a_width
    col_k = col_q + key_dim
    col_v = col_k + key_dim
    col_r = col_v + val_dim
    chunk_p = min(sgu_chunk, len_p)
    chunk_s = min(sgu_chunk, len_s)

    x_p = x_prompt.reshape(rows_p, d_model)
    x_s = x_sample.reshape(rows_s, d_model)
    zero_state = jnp.zeros((n_p, heads_b, dk, dv), state_gla.dtype)

    gla_p, gla_s, vn_s = [], [], []
    for d in range(depth):
        w_up = jnp.pad(w_gate_up[d], ((0, LANES - rank), (0, 0))).astype(BF16)
        head_dim = a_width // a_heads
        bias_full = jnp.repeat(b_s[d].T, head_dim, axis=1)
        row = lambda v: v.reshape(1, -1)

        w_in_t = w_in[d].T
        h, b_all = _prep(x_p, x_s, row(g_mix[d]), w_in_t, main_cols, rank, w_up, row(b_gate[d]))
        q_scale = dk ** -0.5
        proj, w_o = _in_proj(h, w_in_t, main_cols, (
            (0, col_q, _gelu_tanh),
            (col_q, col_k, lambda t: t * q_scale),
            (col_k, col_r, lambda t: t),
            (col_r, main_cols, _silu),
        ), ride=w_out[d])

        cols = dict(col_q=col_q, col_k=col_k, col_v=col_v, col_r=col_r)
        shared = (row(gla_norm_g[d]),)
        ln = (row(ln_g[d]), row(ln_b[d]))
        a_p, o_p, s_p = _mixers(
            proj, b_all, zero_state, *shared, w_s[d][:, :chunk_p, :chunk_p], bias_full[:chunk_p],
            *ln, 0, n_p, len_p, **cols, sgu_chunk=chunk_p, emit_vn=False)
        a_s, o_s, s_s, vn = _mixers(
            proj, b_all, state_gla[d], *shared, w_s[d][:, :chunk_s, :chunk_s], bias_full[:chunk_s],
            *ln, rows_p, n_s, len_s, **cols, sgu_chunk=chunk_s, emit_vn=True)

        x1, xg, ssq = _mix(a_p, a_s, o_p, o_s, x_p, x_s, w_o, row(g_ffn[d]))
        h1, wd = _ffn_up(xg, ssq, w_ffn_gate[d], w_ffn_up[d], ride=w_ffn_down[d])
        if d + 1 < depth:
            x = _ffn_down(h1, wd, x1)
            x_p, x_s = x[:rows_p], x[rows_p:]
        else:
            gf = g_final.reshape(1, -1)
            y_p = _ffn_down_norm(h1, wd, x1, gf, 0, rows_p).reshape(n_p, len_p, d_model)
            y_s = _ffn_down_norm(h1, wd, x1, gf, rows_p, rows_s).reshape(n_s, len_s, d_model)

        gla_p.append(s_p)
        gla_s.append(s_s)
        vn_s.append(vn.reshape(n_s, len_s, a_width))

    return (y_p, y_s, jnp.stack(gla_p), jnp.stack(gla_s), jnp.stack(vn_s))
```

```python
import functools
import math

import jax
import jax.numpy as jnp
from jax import lax
from jax.experimental import pallas as pl
from jax.experimental.pallas import tpu as pltpu

F32 = jnp.float32
BF16 = jnp.bfloat16

EPS = 1e-6
GLA_BLOCK = 64
GATE_TAU = 16.0
LANES = 128
BF16_SUBLANES = 16
V7X_MXU_DIM = 256
V7X_VMEM_LIMIT_BYTES = 56 * 1024 * 1024


def _pick(n, candidates):
    for c in candidates:
        if n % c == 0:
            return c
    raise ValueError(f"no tile in {candidates} divides {n}")


def _params(*semantics):
    return pltpu.CompilerParams(dimension_semantics=semantics,
                                vmem_limit_bytes=V7X_VMEM_LIMIT_BYTES)


def _dot(a, b):
    return jnp.dot(a, b, preferred_element_type=F32)


def _gelu_tanh(x):
    c = math.sqrt(2.0 / math.pi)
    return 0.5 * x * (1.0 + jnp.tanh(c * (x + 0.044715 * (x * x * x))))


def _silu(x):
    return x * (1.0 / (1.0 + jnp.exp(-x)))


def _rms(x, g):
    return x * lax.rsqrt(jnp.mean(x * x, axis=-1, keepdims=True) + EPS) * g


def _prep_kernel(xp_ref, xs_ref, g_ref, win_ref, wup_ref, bg_ref, tri_ref, h_ref, b_ref, wlr_ref,
                 *, p_tiles, sub, rank):
    @pl.when(pl.program_id(0) == 0)
    def _():
        pad = jnp.zeros((LANES - rank, win_ref.shape[1]), F32)
        wlr_ref[...] = jnp.transpose(jnp.concatenate([win_ref[...], pad], axis=0)).astype(BF16)

    def body(x_ref):
        for c in range(x_ref.shape[0] // sub):
            rs = slice(c * sub, (c + 1) * sub)
            hb = _rms(x_ref[rs, :], g_ref[...]).astype(BF16)
            h_ref[rs, :] = hb
            g_lr = _dot(hb, wlr_ref[...])
            z = _dot(g_lr.astype(BF16), wup_ref[...]) + bg_ref[...]
            log_a = (jnp.minimum(z, 0.0) - jnp.log1p(jnp.exp(-jnp.abs(z)))) * (1.0 / GATE_TAU)
            hi = log_a.astype(BF16)
            r1 = log_a - hi.astype(F32)
            mid = r1.astype(BF16)
            lo = (r1 - mid.astype(F32)).astype(BF16)
            tri = tri_ref[...]
            b_ref[rs, :] = _dot(tri, hi) + _dot(tri, mid) + _dot(tri, lo)

    _on_prompt_or_sample(pl.program_id(0), p_tiles, body, (xp_ref,), (xs_ref,))


def _on_prompt_or_sample(i, p_tiles, body, prompt_refs, sample_refs):
    @pl.when(i < p_tiles)
    def _():
        body(*prompt_refs)

    @pl.when(i >= p_tiles)
    def _():
        body(*sample_refs)


def _split_specs(block, p_tiles, col_tiles=None, single_buffer_sample=False):
    def prompt_map(i, *g):
        col = jnp.where(i < p_tiles, g[0], col_tiles - 1) if col_tiles else 0
        return jnp.minimum(i, p_tiles - 1), col

    def sample_map(i, *g):
        col = jnp.where(i >= p_tiles, g[0], 0) if col_tiles else 0
        return jnp.maximum(i - p_tiles, 0), col

    mode = dict(pipeline_mode=pl.Buffered(1)) if single_buffer_sample else {}
    return [pl.BlockSpec(block, prompt_map), pl.BlockSpec(block, sample_map, **mode)]


def _prep(x_p, x_s, g, w_in_t, gate_row0, rank, w_up, b_gate):
    (n_p, d), n_s = x_p.shape, x_s.shape[0]
    n = n_p + n_s
    key_dim = w_up.shape[1]
    assert gate_row0 % rank == 0 and rank % 8 == 0 and rank <= LANES
    bm = _pick(math.gcd(n_p, n_s), (512, 256, 128, 64))
    sub = _pick(bm, (128, 64))
    idx = jnp.arange(sub)
    tri = ((idx[None, :] <= idx[:, None])
           & (idx[None, :] // GLA_BLOCK == idx[:, None] // GLA_BLOCK)).astype(BF16)
    p_tiles = n_p // bm
    const = lambda shape: pl.BlockSpec(shape, lambda i: (0, 0))
    return pl.pallas_call(
        functools.partial(_prep_kernel, p_tiles=p_tiles, sub=sub, rank=rank),
        grid=(n // bm,),
        in_specs=_split_specs((bm, d), p_tiles, single_buffer_sample=True) + [
            const((1, d)),
            pl.BlockSpec((rank, d), lambda i: (gate_row0 // rank, 0)),
            const(w_up.shape), const((1, key_dim)), const((sub, sub)),
        ],
        out_specs=[
            pl.BlockSpec((bm, d), lambda i: (i, 0)),
            pl.BlockSpec((bm, key_dim), lambda i: (i, 0)),
        ],
        out_shape=[
            jax.ShapeDtypeStruct((n, d), BF16),
            jax.ShapeDtypeStruct((n, key_dim), F32),
        ],
        scratch_shapes=[pltpu.VMEM((d, LANES), BF16)],
        compiler_params=_params("arbitrary"),
        name="prep_rmsnorm_gate",
    )(x_p, x_s, g, w_in_t, w_up, b_gate, tri)


class _WeightStream:
    def __init__(self, w_hbms, wbuf, stage, sem, *, bn, n_col_steps, first_col, first_width,
                 next_col, transposed=False):
        self.w_hbms, self.wbuf, self.stage, self.sem = w_hbms, wbuf, stage, sem
        self.transposed = transposed
        self.bn = bn
        self.rows = stage.shape[2]
        self.tile_rows = wbuf.shape[2] - self.rows
        self.n_chunks = self.tile_rows // self.rows
        self.n_col_steps = n_col_steps
        self.first_col, self.first_width = first_col, first_width
        self.next_col = next_col
        self.j = pl.program_id(0)
        self.i = pl.program_id(1)

    def _copy(self, m, chunk, col0, width, slot):
        if self.transposed:
            assert width == self.bn
            src = self.w_hbms[m].at[pl.ds(col0 + chunk * self.rows, self.rows), :]
            dst = self.stage.at[m, slot]
        else:
            src = self.w_hbms[m].at[pl.ds(chunk * self.rows, self.rows), pl.ds(col0, width)]
            dst = self.stage.at[m, slot, :, pl.ds(0, width)]
        return pltpu.make_async_copy(src, dst, self.sem.at[m, slot])

    def _next_copy(self, m, chunk):
        col0 = pl.multiple_of(self.next_col(self.j) * self.bn, self.bn)
        return self._copy(m, chunk, col0, self.bn, chunk % 2)

    def advance(self):
        j, i, n_mats = self.j, self.i, len(self.w_hbms)

        @pl.when((j == 0) & (i == 0))
        def _():
            self.stage[...] = jnp.zeros_like(self.stage)
            for c in range(self.n_chunks):
                for m in range(n_mats):
                    cp = self._copy(m, c, self.first_col * self.bn, self.first_width, c % 2)
                    cp.start()
                    cp.wait()
                    self._store_chunk(m, 0, c, c % 2)

        has_next = j + 1 < self.n_col_steps

        @pl.when(has_next & (i >= 1) & (i <= self.n_chunks))
        def _():
            for m in range(n_mats):
                self._next_copy(m, i - 1).wait()

        @pl.when(has_next & (i < self.n_chunks))
        def _():
            for m in range(n_mats):
                self._next_copy(m, i).start()

    def _store_chunk(self, m, tile_slot, chunk, stage_slot):
        row0 = chunk * self.rows
        if not isinstance(row0, int):
            row0 = pl.multiple_of(row0, self.rows)
        self.wbuf[m, tile_slot, pl.ds(row0, self.rows), :] = self.stage[m, stage_slot].astype(BF16)

    def cast_arrived_chunk(self):
        j, i = self.j, self.i
        valid = (j + 1 < self.n_col_steps) & (i >= 1) & (i <= self.n_chunks)
        chunk = jnp.where(valid, i - 1, self.n_chunks)
        for m in range(len(self.w_hbms)):
            self._store_chunk(m, (j + 1) % 2, chunk, (i + 1) % 2)

    def tile(self, m, width=None):
        w = self.wbuf[m, self.j % 2, :self.tile_rows, :]
        return w if width is None else w[:, :width]


class _CastRider:
    def __init__(self, src_hbm, dst_hbm, in_buf, out_buf, sem, step):
        self.src, self.dst, self.in_buf, self.out_buf, self.sem = src_hbm, dst_hbm, in_buf, out_buf, sem
        self.rows = in_buf.shape[1]
        self.n_chunks = src_hbm.shape[0] // self.rows
        self.step = step

    def _fetch(self, chunk):
        rs = pl.ds(pl.multiple_of(chunk * self.rows, self.rows), self.rows)
        return pltpu.make_async_copy(self.src.at[rs, :], self.in_buf.at[chunk % 2],
                                     self.sem.at[0, chunk % 2])

    def _write_back(self, chunk):
        rs = pl.ds(pl.multiple_of(chunk * self.rows, self.rows), self.rows)
        return pltpu.make_async_copy(self.out_buf.at[chunk % 2], self.dst.at[rs, :],
                                     self.sem.at[1, chunk % 2])

    def advance(self):
        g, n = self.step, self.n_chunks

        @pl.when(g == 0)
        def _():
            self.in_buf[...] = jnp.zeros_like(self.in_buf)

        steady = (g >= 3) & (g < n)

        @pl.when(steady)
        def _():
            self._write_back(g - 3).wait()
            self._fetch(g - 1).wait()
            self._fetch(g).start()
            self._write_back(g - 2).start()

        @pl.when(jnp.logical_not(steady))
        def _():
            pl.when((g >= 3) & (g < n + 3))(lambda: self._write_back(g - 3).wait())
            pl.when((g >= 1) & (g < n + 1))(lambda: self._fetch(g - 1).wait())
            pl.when(g < n)(lambda: self._fetch(g).start())
            pl.when((g >= 2) & (g < n + 2))(lambda: self._write_back(g - 2).start())

    def cast_arrived_chunk(self):
        slot = (self.step + 1) % 2
        self.out_buf[slot] = self.in_buf[slot].astype(BF16)


def _cast_rider_scratch(src_shape, n_steps):
    n_rows, n_cols = src_shape
    rows = BF16_SUBLANES
    while n_rows % rows or n_rows // rows + 3 > n_steps:
        rows += BF16_SUBLANES
        assert rows <= n_rows, "too few grid steps to ride the cast on"
    return [
        pltpu.VMEM((2, rows, n_cols), F32),
        pltpu.VMEM((2, rows, n_cols), BF16),
        pltpu.SemaphoreType.DMA((2, 2)),
    ]


_MAX_WEIGHT_CHUNKS = 16


def _weight_stream_scratch(n_mats, k, bn, n_row_steps, transposed=False):
    assert n_row_steps >= 2, "weight chunks are prefetched across the row steps of one column tile"
    tile_rows, tile_cols = (bn, k) if transposed else (k, bn)
    n_chunks = 1
    while (n_chunks * 2 <= min(n_row_steps - 1, _MAX_WEIGHT_CHUNKS)
           and tile_rows % (n_chunks * 2 * BF16_SUBLANES) == 0):
        n_chunks *= 2
    rows = tile_rows // n_chunks
    return [
        pltpu.VMEM((n_mats, 2, tile_rows + rows, tile_cols), BF16),
        pltpu.VMEM((n_mats, 2, rows, tile_cols), F32),
        pltpu.SemaphoreType.DMA((n_mats, 2)),
    ]


def _in_proj_kernel(a_ref, wt_hbm, ride_hbm, o_ref, rode_hbm, wbuf, stage, sem,
                    ride_in, ride_out, ride_sem, *, epilogues, n_col_steps):
    bn = o_ref.shape[1]
    ws = _WeightStream([wt_hbm], wbuf, stage, sem, bn=bn, n_col_steps=n_col_steps,
                       first_col=0, first_width=bn, next_col=lambda j: j + 1, transposed=True)
    rider = _CastRider(ride_hbm, rode_hbm, ride_in, ride_out, ride_sem,
                       ws.j * pl.num_programs(1) + ws.i)
    ws.advance()
    rider.advance()
    for first, end, fn in epilogues:
        @pl.when((ws.j >= first) & (ws.j < end))
        def _(fn=fn):
            ws.cast_arrived_chunk()
            rider.cast_arrived_chunk()
            acc = lax.dot_general(a_ref[...], ws.tile(0), (((1,), (1,)), ((), ())),
                                  preferred_element_type=F32)
            o_ref[...] = fn(acc).astype(o_ref.dtype)


def _in_proj(h, w_t, n_cols, col_epilogues, ride):
    m, k = h.shape
    bm = _pick(m, (1024, 512, 256, 128, 64))
    bn = _pick(math.gcd(*[end for _, end, _ in col_epilogues]), (1024, 512, 256, 128))
    epilogues = tuple((first // bn, end // bn, fn) for first, end, fn in col_epilogues)
    n_col_steps = n_cols // bn
    n_row_steps = m // bm
    anywhere = pl.BlockSpec(memory_space=pl.ANY)
    return pl.pallas_call(
        functools.partial(_in_proj_kernel, epilogues=epilogues, n_col_steps=n_col_steps),
        grid=(n_col_steps, n_row_steps),
        in_specs=[pl.BlockSpec((bm, k), lambda j, i: (i, 0)), anywhere, anywhere],
        out_specs=[pl.BlockSpec((bm, bn), lambda j, i: (i, j)), anywhere],
        out_shape=[jax.ShapeDtypeStruct((m, n_cols), BF16),
                   jax.ShapeDtypeStruct(ride.shape, BF16)],
        scratch_shapes=(_weight_stream_scratch(1, k, bn, n_row_steps, transposed=True)
                        + _cast_rider_scratch(ride.shape, n_col_steps * n_row_steps)),
        compiler_params=_params("arbitrary", "arbitrary"),
        name="gemm_in_proj",
    )(h, w_t, ride)


def _sgu_tile(u_ref, v_ref, wm_ref, bias_ref, lng_ref, lnb_ref, o_ref, vn_ref):
    rows, width = u_ref.shape
    heads, chunk, _ = wm_ref.shape
    head_dim = width // heads
    n_chunks = rows // chunk
    v = v_ref[...].astype(F32)
    mu = jnp.mean(v, axis=-1, keepdims=True)
    vc = v - mu
    var = jnp.mean(vc * vc, axis=-1, keepdims=True)
    vn = vc * lax.rsqrt(var + EPS) * lng_ref[...] + lnb_ref[...]
    if vn_ref is not None:
        vn_ref[...] = vn
    vnb = vn.astype(BF16)
    for h in range(heads):
        lo, hi = h * head_dim, (h + 1) * head_dim
        rhs = jnp.concatenate(
            [vnb[c * chunk:(c + 1) * chunk, lo:hi] for c in range(n_chunks)], axis=1)
        mixed = _dot(wm_ref[h], rhs)
        bias = bias_ref[:, lo:hi]
        for c in range(n_chunks):
            rs = slice(c * chunk, (c + 1) * chunk)
            u = u_ref[rs, lo:hi].astype(F32)
            m = mixed[:, c * head_dim:(c + 1) * head_dim] + bias
            o_ref[rs, lo:hi] = (u * m).astype(o_ref.dtype)


def _mixers_kernel(u_ref, vg_ref, q_ref, k_ref, v_ref, r_ref, b_ref, s0_ref, gn_ref,
                   w_ref, bias_ref, lng_ref, lnb_ref, a_ref, o_ref, s_out_ref, *rest,
                   heads, dk, dv, blocks, emit_vn):
    vn_ref, s_ref, wm_ref = rest if emit_vn else (None,) + rest
    t = pl.program_id(1)

    @pl.when((pl.program_id(0) == 0) & (t == 0))
    def _():
        chunk = w_ref.shape[1]
        ri = lax.broadcasted_iota(jnp.int32, (chunk, chunk), 0)
        ci = lax.broadcasted_iota(jnp.int32, (chunk, chunk), 1)
        for h in range(w_ref.shape[0]):
            wm_ref[h] = jnp.where(ci <= ri, w_ref[h], 0.0).astype(BF16)

    @pl.when(t == 0)
    def _():
        s_ref[...] = s0_ref[0]

    _sgu_tile(u_ref, vg_ref, wm_ref, bias_ref, lng_ref, lnb_ref, a_ref, vn_ref)

    ri = lax.broadcasted_iota(jnp.int32, (GLA_BLOCK, GLA_BLOCK), 0)
    ci = lax.broadcasted_iota(jnp.int32, (GLA_BLOCK, GLA_BLOCK), 1)
    causal = ci <= ri
    half = GLA_BLOCK // 2
    for blk in range(blocks):
        rs = slice(blk * GLA_BLOCK, (blk + 1) * GLA_BLOCK)
        for h in range(heads):
            ks = slice(h * dk, (h + 1) * dk)
            vs = slice(h * dv, (h + 1) * dv)
            q = q_ref[rs, ks].astype(F32)
            k = k_ref[rs, ks].astype(F32)
            v = v_ref[rs, vs]
            b = b_ref[rs, ks]
            b_mid = b[half:half + 1, :]
            b_last = b[GLA_BLOCK - 1:GLA_BLOCK, :]
            q_in = (q * jnp.exp(b - b_mid)).astype(BF16)
            k_in = (k * jnp.exp(b_mid - b)).astype(BF16)
            att = lax.dot_general(q_in, k_in, (((1,), (1,)), ((), ())),
                                  preferred_element_type=F32)
            att = jnp.where(causal, att, 0.0).astype(BF16)
            s = s_ref[h]
            o = _dot(att, v) + _dot((q * jnp.exp(b)).astype(BF16), s.astype(BF16))
            k_out = (k * jnp.exp(b_last - b)).astype(BF16)
            kv = lax.dot_general(k_out, v, (((0,), (0,)), ((), ())),
                                 preferred_element_type=F32)
            decay_col = jnp.transpose(
                jnp.broadcast_to(jnp.exp(b_last), (LANES, dk)))
            decay = jnp.concatenate([decay_col] * (dv // LANES), axis=1)
            s_ref[h] = decay * s + kv
            o_n = _rms(o, gn_ref[...])
            o_ref[rs, vs] = (o_n * r_ref[rs, vs].astype(F32)).astype(o_ref.dtype)

    @pl.when(t == pl.num_programs(1) - 1)
    def _():
        s_out_ref[0] = s_ref[...]


def _mixers(proj, b_all, s0, gn, w_s, bias_full, ln_g, ln_b, row0, n_streams, stream_len, *,
            col_q, col_k, col_v, col_r, sgu_chunk, emit_vn):
    _, heads, dk, dv = s0.shape
    kd, vd = heads * dk, heads * dv
    a_width = ln_g.shape[1]
    blocks = _pick(stream_len // GLA_BLOCK, (4, 2, 1))
    rows = blocks * GLA_BLOCK
    assert rows % sgu_chunk == 0, "an SGU chunk may not straddle two row tiles"
    steps = stream_len // rows
    r0 = row0 // rows
    cq, ck, cv, cr = col_q // kd, col_k // kd, col_v // vd, col_r // vd
    assert w_s.shape[1:] == (sgu_chunk, sgu_chunk)
    kern = functools.partial(_mixers_kernel, heads=heads, dk=dk, dv=dv, blocks=blocks,
                             emit_vn=emit_vn)
    row = lambda s, t: r0 + s * steps + t
    out_row = lambda s, t: s * steps + t
    n_rows = n_streams * stream_len
    const = lambda shape: pl.BlockSpec(shape, lambda s, t: (0,) * len(shape))
    out_specs = [
        pl.BlockSpec((rows, a_width), lambda s, t: (out_row(s, t), 0)),
        pl.BlockSpec((rows, vd), lambda s, t: (out_row(s, t), 0)),
        pl.BlockSpec((1, heads, dk, dv), lambda s, t: (s, 0, 0, 0)),
    ]
    out_shape = [
        jax.ShapeDtypeStruct((n_rows, a_width), BF16),
        jax.ShapeDtypeStruct((n_rows, vd), BF16),
        jax.ShapeDtypeStruct(s0.shape, F32),
    ]
    if emit_vn:
        out_specs.append(pl.BlockSpec((rows, a_width), lambda s, t: (out_row(s, t), 0)))
        out_shape.append(jax.ShapeDtypeStruct((n_rows, a_width), F32))
    return pl.pallas_call(
        kern,
        grid=(n_streams, steps),
        in_specs=[
            pl.BlockSpec((rows, a_width), lambda s, t: (row(s, t), 0)),
            pl.BlockSpec((rows, a_width), lambda s, t: (row(s, t), 1)),
            pl.BlockSpec((rows, kd), lambda s, t: (row(s, t), cq)),
            pl.BlockSpec((rows, kd), lambda s, t: (row(s, t), ck)),
            pl.BlockSpec((rows, vd), lambda s, t: (row(s, t), cv)),
            pl.BlockSpec((rows, vd), lambda s, t: (row(s, t), cr)),
            pl.BlockSpec((rows, kd), lambda s, t: (row(s, t), 0)),
            pl.BlockSpec((1, heads, dk, dv), lambda s, t: (s, 0, 0, 0)),
            const((1, dv)), const(w_s.shape), const(bias_full.shape),
            const((1, a_width)), const((1, a_width)),
        ],
        out_specs=out_specs,
        out_shape=out_shape,
        scratch_shapes=[pltpu.VMEM((heads, dk, dv), F32), pltpu.VMEM(w_s.shape, BF16)],
        compiler_params=_params("arbitrary", "arbitrary"),
        name="sgu_gla_mixers",
    )(proj, proj, proj, proj, proj, proj, b_all, s0, gn, w_s, bias_full, ln_g, ln_b)


def _mix_kernel(ap_ref, as_ref, bp_ref, bs_ref, xp_ref, xs_ref, wa_ref, wb_ref, g_ref,
                x1_ref, xg_ref, ssq_ref, *, p_tiles):
    j = pl.program_id(1)

    def body(a_ref, b_ref, x_ref):
        half = x_ref.shape[0] // 2
        for rs in (slice(0, half), slice(half, 2 * half)):
            x1 = x_ref[rs, :] + (_dot(a_ref[rs, :], wa_ref[...]) + _dot(b_ref[rs, :], wb_ref[...]))
            x1_ref[rs, :] = x1
            xg_ref[rs, :] = (x1 * g_ref[...]).astype(xg_ref.dtype)
            sq = x1 * x1
            part = sq[:, :LANES]
            for c in range(1, sq.shape[1] // LANES):
                part = part + sq[:, c * LANES:(c + 1) * LANES]
            ssq_ref[rs, :] += part

    @pl.when(j == 0)
    def _():
        ssq_ref[...] = jnp.zeros_like(ssq_ref)

    _on_prompt_or_sample(pl.program_id(0), p_tiles, body,
                         (ap_ref, bp_ref, xp_ref), (as_ref, bs_ref, xs_ref))


def _mix(a_p, a_s, b_p, b_s, x_p, x_s, w_out, g):
    (n_p, ka), n_s = a_p.shape, a_s.shape[0]
    m = n_p + n_s
    kb = b_p.shape[1]
    n = w_out.shape[1]
    assert ka == kb, "head groups of different widths need separate weight specs"
    bm = _pick(math.gcd(n_p, n_s), (1024, 512, 256, 128, 64))
    bn = _pick(n, (512, 256, 128))
    p_tiles = n_p // bm
    split = functools.partial(_split_specs, p_tiles=p_tiles, single_buffer_sample=True)
    return pl.pallas_call(
        functools.partial(_mix_kernel, p_tiles=p_tiles),
        grid=(m // bm, n // bn),
        in_specs=(split((bm, ka)) + split((bm, kb)) + split((bm, bn), col_tiles=n // bn) + [
            pl.BlockSpec((ka, bn), lambda i, j: (0, j)),
            pl.BlockSpec((kb, bn), lambda i, j: (1, j)),
            pl.BlockSpec((1, bn), lambda i, j: (0, j)),
        ]),
        out_specs=[
            pl.BlockSpec((bm, bn), lambda i, j: (i, j)),
            pl.BlockSpec((bm, bn), lambda i, j: (i, j)),
            pl.BlockSpec((bm, LANES), lambda i, j: (i, 0)),
        ],
        out_shape=[
            jax.ShapeDtypeStruct((m, n), F32),
            jax.ShapeDtypeStruct((m, n), BF16),
            jax.ShapeDtypeStruct((m, LANES), F32),
        ],
        compiler_params=_params("parallel", "arbitrary"),
        name="gemm_out_proj_residual",
    )(a_p, a_s, b_p, b_s, x_p, x_s, w_out, w_out, g)


def _ffn_up_kernel(xg_ref, ssq_ref, wg_hbm, wu_hbm, ride_hbm, o_ref, rode_hbm, wbuf, stage, sem,
                   ride_in, ride_out, ride_sem, *, d_model, d_ff):
    bn = o_ref.shape[1]
    n_steps = -(-d_ff // bn)
    edge = d_ff - (n_steps - 1) * bn
    edge_first = edge != bn
    ws = _WeightStream([wg_hbm, wu_hbm], wbuf, stage, sem, bn=bn, n_col_steps=n_steps,
                       first_col=n_steps - 1 if edge_first else 0, first_width=edge,
                       next_col=(lambda j: j) if edge_first else (lambda j: j + 1))
    rider = _CastRider(ride_hbm, rode_hbm, ride_in, ride_out, ride_sem,
                       ws.j * pl.num_programs(1) + ws.i)
    ws.advance()
    rider.advance()

    def body(width):
        ws.cast_arrived_chunk()
        rider.cast_arrived_chunk()
        rinv = lax.rsqrt(jnp.sum(ssq_ref[...], axis=-1, keepdims=True) * (1.0 / d_model) + EPS)
        h = xg_ref[...]
        g = _dot(h, ws.tile(0, width)) * rinv
        u = _dot(h, ws.tile(1, width)) * rinv
        o_ref[:, :width] = (_silu(g) * u).astype(o_ref.dtype)

    if edge_first:
        pl.when(ws.j == 0)(lambda: body(edge))
        pl.when(ws.j > 0)(lambda: body(bn))
    else:
        body(bn)


def _ffn_up(xg, ssq, wg, wu, ride):
    m, k = xg.shape
    f = wg.shape[1]
    bm = _pick(m, (1024, 512, 256, 128, 64))
    bn = 512
    assert (f % bn) % V7X_MXU_DIM == 0, "the edge column block must stay MXU-column aligned"
    n_steps = pl.cdiv(f, bn)
    n_row_steps = m // bm
    col = (lambda j: j) if f % bn == 0 else (lambda j: jnp.where(j == 0, n_steps - 1, j - 1))
    anywhere = pl.BlockSpec(memory_space=pl.ANY)
    return pl.pallas_call(
        functools.partial(_ffn_up_kernel, d_model=k, d_ff=f),
        grid=(n_steps, n_row_steps),
        in_specs=[
            pl.BlockSpec((bm, k), lambda j, i: (i, 0)),
            pl.BlockSpec((bm, LANES), lambda j, i: (i, 0)),
            anywhere, anywhere, anywhere,
        ],
        out_specs=[pl.BlockSpec((bm, bn), lambda j, i: (i, col(j))), anywhere],
        out_shape=[jax.ShapeDtypeStruct((m, f), BF16), jax.ShapeDtypeStruct(ride.shape, BF16)],
        scratch_shapes=(_weight_stream_scratch(2, k, bn, n_row_steps)
                        + _cast_rider_scratch(ride.shape, n_steps * n_row_steps)),
        compiler_params=_params("arbitrary", "arbitrary"),
        name="ffn_gate_up",
    )(xg, ssq, wg, wu, ride)


def _ffn_down_kernel(h_ref, w_ref, x_ref, o_ref, *, d_ff):
    k = pl.program_id(2)
    bk = h_ref.shape[1]
    n_steps = -(-d_ff // bk)
    last = d_ff - (n_steps - 1) * bk

    def body(width, first):
        p = _dot(h_ref[:, :width], w_ref[:width, :])
        if first:
            o_ref[...] = x_ref[...] + p
        else:
            o_ref[...] += p

    if n_steps == 1:
        body(last, True)
        return
    pl.when(k == 0)(lambda: body(bk, True))
    if n_steps > 2:
        pl.when((k > 0) & (k < n_steps - 1))(lambda: body(bk, False))
    pl.when(k == n_steps - 1)(lambda: body(last, False))


def _ffn_down(h1, wd, x1):
    m, f = h1.shape
    n = wd.shape[1]
    bm = _pick(m, (1024, 512, 256, 128, 64))
    bn = _pick(n, (1024, 512, 256, 128))
    bk = 2816
    assert (f % bk) % V7X_MXU_DIM == 0, "the last contraction block must stay MXU-row aligned"
    return pl.pallas_call(
        functools.partial(_ffn_down_kernel, d_ff=f),
        grid=(m // bm, n // bn, pl.cdiv(f, bk)),
        in_specs=[
            pl.BlockSpec((bm, bk), lambda i, j, k: (i, k)),
            pl.BlockSpec((bk, bn), lambda i, j, k: (k, j)),
            pl.BlockSpec((bm, bn), lambda i, j, k: (i, j)),
        ],
        out_specs=pl.BlockSpec((bm, bn), lambda i, j, k: (i, j)),
        out_shape=jax.ShapeDtypeStruct((m, n), F32),
        compiler_params=_params("parallel", "parallel", "arbitrary"),
        name="ffn_down_residual",
    )(h1, wd, x1)


def _ffn_down_norm_kernel(h_ref, w_ref, x_ref, g_ref, y_ref, *, d_ff):
    k = pl.program_id(1)
    bk = h_ref.shape[1]
    n_steps = -(-d_ff // bk)
    last = d_ff - (n_steps - 1) * bk

    def body(width, first, final):
        p = _dot(h_ref[:, :width], w_ref[:width, :])
        if first:
            y_ref[...] = x_ref[...] + p
        else:
            y_ref[...] += p
        if final:
            chunk = LANES
            for r in range(0, y_ref.shape[0], chunk):
                y_ref[r:r + chunk, :] = _rms(y_ref[r:r + chunk, :], g_ref[...])

    if n_steps == 1:
        body(last, True, True)
        return
    pl.when(k == 0)(lambda: body(bk, True, False))
    if n_steps > 2:
        pl.when((k > 0) & (k < n_steps - 1))(lambda: body(bk, False, False))
    pl.when(k == n_steps - 1)(lambda: body(last, False, True))


def _ffn_down_norm(h1, wd, x1, g, row0, n_rows):
    f = h1.shape[1]
    n = wd.shape[1]
    bm = _pick(math.gcd(n_rows, row0) if row0 else n_rows, (512, 256, 128, 64))
    bk = 1024
    assert (f % bk) % V7X_MXU_DIM == 0, "the last contraction block must stay MXU-row aligned"
    r0 = row0 // bm
    return pl.pallas_call(
        functools.partial(_ffn_down_norm_kernel, d_ff=f),
        grid=(n_rows // bm, pl.cdiv(f, bk)),
        in_specs=[
            pl.BlockSpec((bm, bk), lambda i, k: (r0 + i, k)),
            pl.BlockSpec((bk, n), lambda i, k: (k, 0)),
            pl.BlockSpec((bm, n), lambda i, k: (r0 + i, 0)),
            pl.BlockSpec((1, n), lambda i, k: (0, 0)),
        ],
        out_specs=pl.BlockSpec((bm, n), lambda i, k: (i, 0)),
        out_shape=jax.ShapeDtypeStruct((n_rows, n), F32),
        compiler_params=_params("parallel", "arbitrary"),
        name="ffn_down_final_norm",
    )(h1, wd, x1, g)


def kernel(x_prompt, x_sample, state_gla, g_mix, w_in, w_s, b_s, ln_g, ln_b, w_gate_up, b_gate,
           gla_norm_g, w_out, g_ffn, w_ffn_gate, w_ffn_up, w_ffn_down, g_final):
    depth = g_mix.shape[0]
    n_p, len_p, d_model = x_prompt.shape
    n_s, len_s, _ = x_sample.shape
    rows_p, rows_s = n_p * len_p, n_s * len_s
    _, _, heads_b, dk, dv = state_gla.shape
    a_width = ln_g.shape[-1]
    a_heads, sgu_chunk = w_s.shape[1], w_s.shape[2]
    key_dim, val_dim = heads_b * dk, heads_b * dv
    rank = w_gate_up.shape[1]
    d_ff = w_ffn_gate.shape[-1]
    main_cols = 2 * a_width + 2 * key_dim + 2 * val_dim
    col_q = 2 * a_width
    col_k = col_q + key_dim
    col_v = col_k + key_dim
    col_r = col_v + val_dim
    chunk_p = min(sgu_chunk, len_p)
    chunk_s = min(sgu_chunk, len_s)

    x_p = x_prompt.reshape(rows_p, d_model)
    x_s = x_sample.reshape(rows_s, d_model)
    zero_state = jnp.zeros((n_p, heads_b, dk, dv), state_gla.dtype)

    gla_p, gla_s, vn_s = [], [], []
    for d in range(depth):
        w_up = jnp.pad(w_gate_up[d], ((0, LANES - rank), (0, 0))).astype(BF16)
        head_dim = a_width // a_heads
        bias_full = jnp.repeat(b_s[d].T, head_dim, axis=1)
        row = lambda v: v.reshape(1, -1)

        w_in_t = w_in[d].T
        h, b_all = _prep(x_p, x_s, row(g_mix[d]), w_in_t, main_cols, rank, w_up, row(b_gate[d]))
        q_scale = dk ** -0.5
        proj, w_o = _in_proj(h, w_in_t, main_cols, (
            (0, col_q, _gelu_tanh),
            (col_q, col_k, lambda t: t * q_scale),
            (col_k, col_r, lambda t: t),
            (col_r, main_cols, _silu),
        ), ride=w_out[d])

        cols = dict(col_q=col_q, col_k=col_k, col_v=col_v, col_r=col_r)
        shared = (row(gla_norm_g[d]),)
        ln = (row(ln_g[d]), row(ln_b[d]))
        a_p, o_p, s_p = _mixers(
            proj, b_all, zero_state, *shared, w_s[d][:, :chunk_p, :chunk_p], bias_full[:chunk_p],
            *ln, 0, n_p, len_p, **cols, sgu_chunk=chunk_p, emit_vn=False)
        a_s, o_s, s_s, vn = _mixers(
            proj, b_all, state_gla[d], *shared, w_s[d][:, :chunk_s, :chunk_s], bias_full[:chunk_s],
            *ln, rows_p, n_s, len_s, **cols, sgu_chunk=chunk_s, emit_vn=True)

        x1, xg, ssq = _mix(a_p, a_s, o_p, o_s, x_p, x_s, w_o, row(g_ffn[d]))
        h1, wd = _ffn_up(xg, ssq, w_ffn_gate[d], w_ffn_up[d], ride=w_ffn_down[d])
        if d + 1 < depth:
            x = _ffn_down(h1, wd, x1)
            x_p, x_s = x[:rows_p], x[rows_p:]
        else:
            gf = g_final.reshape(1, -1)
            y_p = _ffn_down_norm(h1, wd, x1, gf, 0, rows_p).reshape(n_p, len_p, d_model)
            y_s = _ffn_down_norm(h1, wd, x1, gf, rows_p, rows_s).reshape(n_s, len_s, d_model)

        gla_p.append(s_p)
        gla_s.append(s_s)
        vn_s.append(vn.reshape(n_s, len_s, a_width))

    return (y_p, y_s, jnp.stack(gla_p), jnp.stack(gla_s), jnp.stack(vn_s))
```
